```python
import jax, jax.numpy as jnp
from jax import lax
import numpy as np

D_MODEL = 1024
BATCH = 8
SEQ = 8192
DEPTH = 1

HEAD_DIM = 64
ATTN_WIDTH = D_MODEL // 2
N_Q_HEADS = ATTN_WIDTH // HEAD_DIM
N_KV_HEADS = max(1, N_Q_HEADS // 4)
Q_PER_KV = N_Q_HEADS // N_KV_HEADS
KV_WIDTH = N_KV_HEADS * HEAD_DIM
WINDOW = 128
BLOCK = 128
GMLP_WIDTH = D_MODEL - ATTN_WIDTH
GMLP_GROUPS = GMLP_WIDTH // HEAD_DIM
GMLP_GROUP_DIM = GMLP_WIDTH // GMLP_GROUPS
CHUNK = 128
MIX_WIDTH = ATTN_WIDTH + GMLP_WIDTH
D_FF = 4 * D_MODEL
IN_WIDTH = ATTN_WIDTH + 2 * KV_WIDTH + 2 * GMLP_WIDTH
SPLITS = [ATTN_WIDTH, ATTN_WIDTH + KV_WIDTH, ATTN_WIDTH + 2 * KV_WIDTH,
          ATTN_WIDTH + 2 * KV_WIDTH + GMLP_WIDTH]
ALPHA = (2.0 * DEPTH) ** 0.25
BETA = (8.0 * DEPTH) ** -0.25
LN_EPS = 1e-5
NEG_INF = -1e30

kernel_name = "hybrid_swa_sgu_deepnorm_block"


def layer_norm(x, g, b):
    xf = x.astype(jnp.float32)
    mu = jnp.mean(xf, axis=-1, keepdims=True)
    var = jnp.mean(jnp.square(xf - mu), axis=-1, keepdims=True)
    return ((xf - mu) * lax.rsqrt(var + LN_EPS) * g.astype(jnp.float32)
            + b.astype(jnp.float32)).astype(x.dtype)


def rms_norm(x, g):
    xf = x.astype(jnp.float32)
    ms = jnp.mean(jnp.square(xf), axis=-1, keepdims=True)
    return (xf * lax.rsqrt(ms + LN_EPS) * g.astype(jnp.float32)).astype(x.dtype)


def alibi_slopes():
    i = jnp.arange(1, N_Q_HEADS + 1, dtype=jnp.float32)
    return jnp.exp2(-8.0 * i / N_Q_HEADS)


def banded_window_attention(q, k, v, sink):
    B, S = q.shape[0], q.shape[1]
    nb = S // BLOCK
    qb = q.reshape(B, nb, BLOCK, N_KV_HEADS, Q_PER_KV, HEAD_DIM)
    pad = ((0, 0), (BLOCK, BLOCK), (0, 0))
    kp = jnp.pad(k, pad).reshape(B, nb + 2, BLOCK, N_KV_HEADS, HEAD_DIM)
    vp = jnp.pad(v, pad).reshape(B, nb + 2, BLOCK, N_KV_HEADS, HEAD_DIM)
    kb = jnp.concatenate([kp[:, :-2], kp[:, 1:-1], kp[:, 2:]], axis=2)
    vb = jnp.concatenate([vp[:, :-2], vp[:, 1:-1], vp[:, 2:]], axis=2)
    s = jnp.einsum('bnqgrd,bnkgd->bngrqk', qb, kb,
                   preferred_element_type=jnp.float32) * (HEAD_DIM ** -0.5)
    blk = jnp.arange(nb, dtype=jnp.int32)[:, None] * BLOCK
    q_pos = blk + jnp.arange(BLOCK, dtype=jnp.int32)[None, :]
    k_pos = blk - BLOCK + jnp.arange(3 * BLOCK, dtype=jnp.int32)[None, :]
    dist = jnp.abs(q_pos[:, :, None] - k_pos[:, None, :])
    valid = (dist <= WINDOW) & (k_pos[:, None, :] >= 0) & (k_pos[:, None, :] < S)
    slopes = alibi_slopes().reshape(N_KV_HEADS, Q_PER_KV)
    bias = -slopes[None, :, :, None, None] * dist[:, None, None].astype(jnp.float32)
    s = jnp.where(valid[:, None, None], s + bias, NEG_INF)
    sink_b = sink.astype(jnp.float32).reshape(1, 1, N_KV_HEADS, Q_PER_KV, 1, 1)
    m = jnp.maximum(jnp.max(s, axis=-1, keepdims=True), sink_b)
    e = jnp.exp(s - m)
    p = e / (jnp.sum(e, axis=-1, keepdims=True) + jnp.exp(sink_b - m))
    o = jnp.einsum('bngrqk,bnkgd->bnqgrd', p.astype(vb.dtype), vb)
    return o.reshape(B, S, ATTN_WIDTH)


def chunked_spatial_gating(u, vg, ln_g, ln_b, w_s, b_s):
    B, S = u.shape[0], u.shape[1]
    nc = S // CHUNK
    vn = layer_norm(vg, ln_g, ln_b).reshape(B, nc, CHUNK, GMLP_GROUPS, GMLP_GROUP_DIM)
    mixed = jnp.einsum('gts,bnsgc->bntgc', w_s, vn) + b_s.T[None, None, :, :, None]
    return u * mixed.reshape(B, S, GMLP_WIDTH)


def setup_inputs(seed: int = 0) -> dict:
    key = jax.random.key(seed)
    ks = jax.random.split(key, 16)
    f32 = jnp.float32
    L = DEPTH
    x = jax.random.normal(ks[0], (BATCH, SEQ, D_MODEL), f32)
    w_in = jax.random.normal(ks[1], (L, D_MODEL, IN_WIDTH), f32) * D_MODEL ** -0.5
    w_in = w_in.at[:, :, SPLITS[1]:SPLITS[2]].multiply(BETA)
    sink = 0.5 * jax.random.normal(ks[2], (L, N_Q_HEADS), f32)
    gmlp_ln_g = 1.0 + 0.05 * jax.random.normal(ks[3], (L, GMLP_WIDTH), f32)
    gmlp_ln_b = 0.02 * jax.random.normal(ks[4], (L, GMLP_WIDTH), f32)
    w_spatial = jax.random.normal(ks[5], (L, GMLP_GROUPS, CHUNK, CHUNK), f32) * CHUNK ** -0.5
    b_spatial = 1.0 + 0.1 * jax.random.normal(ks[6], (L, GMLP_GROUPS, CHUNK), f32)
    attn_norm_g = 1.0 + 0.05 * jax.random.normal(ks[7], (L, ATTN_WIDTH), f32)
    gmlp_norm_g = 1.0 + 0.05 * jax.random.normal(ks[8], (L, GMLP_WIDTH), f32)
    w_out = jax.random.normal(ks[9], (L, MIX_WIDTH, D_MODEL), f32) * (MIX_WIDTH ** -0.5 * BETA)
    ln1_g = 1.0 + 0.05 * jax.random.normal(ks[10], (L, D_MODEL), f32)
    ln1_b = 0.02 * jax.random.normal(ks[11], (L, D_MODEL), f32)
    w_ff1 = jax.random.normal(ks[12], (L, D_MODEL, D_FF), f32) * (D_MODEL ** -0.5 * BETA)
    w_ff2 = jax.random.normal(ks[13], (L, D_FF, D_MODEL), f32) * (D_FF ** -0.5 * BETA)
    ln2_g = 1.0 + 0.05 * jax.random.normal(ks[14], (L, D_MODEL), f32)
    ln2_b = 0.02 * jax.random.normal(ks[15], (L, D_MODEL), f32)
    return {"x": x, "w_in": w_in, "sink": sink, "gmlp_ln_g": gmlp_ln_g,
            "gmlp_ln_b": gmlp_ln_b, "w_spatial": w_spatial, "b_spatial": b_spatial,
            "attn_norm_g": attn_norm_g, "gmlp_norm_g": gmlp_norm_g, "w_out": w_out,
            "ln1_g": ln1_g, "ln1_b": ln1_b, "w_ff1": w_ff1, "w_ff2": w_ff2,
            "ln2_g": ln2_g, "ln2_b": ln2_b}


def reference(x, w_in, sink, gmlp_ln_g, gmlp_ln_b, w_spatial, b_spatial,
              attn_norm_g, gmlp_norm_g, w_out, ln1_g, ln1_b, w_ff1, w_ff2,
              ln2_g, ln2_b):
    for l in range(DEPTH):
        proj = jnp.einsum('bsd,de->bse', x, w_in[l])
        q, k, v, gu, gv = jnp.split(proj, SPLITS, axis=-1)
        attn = banded_window_attention(q, k, v, sink[l])
        gu = jax.nn.gelu(gu, approximate=False)
        gv = jax.nn.gelu(gv, approximate=False)
        sgu = chunked_spatial_gating(gu, gv, gmlp_ln_g[l], gmlp_ln_b[l],
                                     w_spatial[l], b_spatial[l])
        mixed = jnp.concatenate([rms_norm(attn, attn_norm_g[l]),
                                 rms_norm(sgu, gmlp_norm_g[l])], axis=-1)
        mix_out = jnp.einsum('bse,ed->bsd', mixed, w_out[l])
        x = layer_norm(ALPHA * x + mix_out, ln1_g[l], ln1_b[l])
        h = jnp.square(jax.nn.relu(jnp.einsum('bsd,df->bsf', x, w_ff1[l])))
        ff_out = jnp.einsum('bsf,fd->bsd', h, w_ff2[l])
        x = layer_norm(ALPHA * x + ff_out, ln2_g[l], ln2_b[l])
    return x
```

```python
import functools

import jax
import jax.numpy as jnp
from jax import lax
from jax.experimental import pallas as pl
from jax.experimental.pallas import tpu as pltpu

HEAD_DIM = 64
N_Q_HEADS = 8
N_KV_HEADS = 2
Q_PER_KV = N_Q_HEADS // N_KV_HEADS
ATTN_WIDTH = N_Q_HEADS * HEAD_DIM
KV_WIDTH = N_KV_HEADS * HEAD_DIM
GMLP_WIDTH = 512
GMLP_GROUPS = 8
WINDOW = 128
BLOCK = 128
LN_EPS = 1e-5
NEG_INF = -1e30
LANES = 128

Q_OFF = 0
K_OFF = ATTN_WIDTH
V_OFF = ATTN_WIDTH + KV_WIDTH
GU_OFF = ATTN_WIDTH + 2 * KV_WIDTH
GV_OFF = GU_OFF + GMLP_WIDTH
IN_WIDTH = GV_OFF + GMLP_WIDTH

MIX_TILE = 512
FFN_TILE = 512
FFN_ROWS = 256
VMEM_LIMIT_BYTES = 56 * 1024 * 1024


def _bdot(a, b):
    return jnp.dot(a, b, preferred_element_type=jnp.float32)


def _gelu(x):
    return 0.5 * x * (1.0 + lax.erf(x * (0.5 ** 0.5)))


def _layer_norm(y, g, b):
    mu = jnp.mean(y, axis=-1, keepdims=True)
    yc = y - mu
    var = jnp.mean(yc * yc, axis=-1, keepdims=True)
    return yc * lax.rsqrt(var + LN_EPS) * g + b


def _rms_norm(y, g):
    ms = jnp.mean(y * y, axis=-1, keepdims=True)
    return y * lax.rsqrt(ms + LN_EPS) * g


def _half_masked(a, left):
    sw = pltpu.roll(a, HEAD_DIM, axis=1)
    zero = jnp.zeros_like(a)
    bf = jnp.bfloat16
    return (jnp.where(left, a, zero).astype(bf), jnp.where(left, zero, sw).astype(bf),
            jnp.where(left, sw, zero).astype(bf), jnp.where(left, zero, a).astype(bf))


def _mix_kernel(sink_ref, x_ref, xp_ref, xn_ref, w_in_ref, lng_ref, lnb_ref, ws_ref, bs_ref,
                ng_ref, w_out_ref, ln1g_ref, ln1b_ref, o_ref,
                q_s, k_s, v_s, gu_s, gv_s, mix_s, *, alpha, n_tiles):
    tile = x_ref.shape[1]
    n_blocks = tile // BLOCK
    seq_tile = pl.program_id(1)
    bf = jnp.bfloat16

    xb = x_ref[0].astype(bf)
    q_s[...] = (_bdot(xb, w_in_ref[:, Q_OFF:K_OFF]) * (HEAD_DIM ** -0.5)).astype(bf)
    gu_s[...] = _gelu(_bdot(xb, w_in_ref[:, GU_OFF:GV_OFF]))
    gv_s[...] = _gelu(_bdot(xb, w_in_ref[:, GV_OFF:IN_WIDTH]))
    kv = _bdot(xb, w_in_ref[:, K_OFF:GU_OFF])
    xh = jnp.concatenate([xp_ref[0], xn_ref[0]], axis=0).astype(bf)
    kvh = _bdot(xh, w_in_ref[:, K_OFF:GU_OFF])

    def store_kv(rows, kv_rows):
        left = lax.broadcasted_iota(jnp.int32, (kv_rows.shape[0], LANES), 1) < HEAD_DIM
        for i, a in enumerate(_half_masked(kv_rows[:, :KV_WIDTH], left)):
            k_s[i, rows, :] = a
        for i, a in enumerate(_half_masked(kv_rows[:, KV_WIDTH:], left)):
            v_s[i, rows, :] = a

    store_kv(pl.ds(0, BLOCK), kvh[:BLOCK])
    store_kv(pl.ds(BLOCK, tile), kv)
    store_kv(pl.ds(BLOCK + tile, BLOCK), kvh[BLOCK:])

    row = lax.broadcasted_iota(jnp.int32, (BLOCK, 3 * BLOCK), 0)
    col = lax.broadcasted_iota(jnp.int32, (BLOCK, 3 * BLOCK), 1)
    dist = jnp.abs(col - BLOCK - row)
    distf = dist.astype(jnp.float32)
    in_window = dist <= WINDOW
    lane_left = lax.broadcasted_iota(jnp.int32, (BLOCK, LANES), 1) < HEAD_DIM

    for j in range(n_blocks):
        rows = pl.ds(j * BLOCK, BLOCK)
        keys = pl.ds(j * BLOCK, 3 * BLOCK)
        valid = in_window
        if j == 0:
            valid = valid & (col >= jnp.where(seq_tile == 0, BLOCK, 0))
        if j == n_blocks - 1:
            valid = valid & (col < jnp.where(seq_tile == n_tiles - 1, 2 * BLOCK, 3 * BLOCK))

        attn_parts = []
        for g in range(N_KV_HEADS):
            lhs = jnp.concatenate(
                [q_s[rows, pl.ds(g * 256, LANES)], q_s[rows, pl.ds(g * 256 + LANES, LANES)]],
                axis=0)
            e_par, inv_par = [], []
            for par in range(2):
                s = lax.dot_general(lhs, k_s[2 * g + par, keys, :],
                                    (((1,), (1,)), ((), ())),
                                    preferred_element_type=jnp.float32)
                e_half, inv_half = [], []
                for half in range(2):
                    h = Q_PER_KV * g + 2 * half + par
                    slope = 2.0 ** (-8.0 * (h + 1) / N_Q_HEADS)
                    sink = sink_ref[h]
                    sh = s[half * BLOCK:(half + 1) * BLOCK]
                    sb = jnp.where(valid, sh - slope * distf, NEG_INF)
                    m = jnp.maximum(jnp.max(sb, axis=-1, keepdims=True), sink)
                    e = jnp.exp(sb - m)
                    den = jnp.sum(e, axis=-1, keepdims=True) + jnp.exp(sink - m)
                    e_half.append(e.astype(bf))
                    inv_half.append(1.0 / den)
                e_par.append(jnp.concatenate(e_half, axis=0))
                inv_par.append(inv_half)
            p = jnp.concatenate(e_par, axis=1)
            vv = jnp.concatenate([v_s[2 * g, keys, :], v_s[2 * g + 1, keys, :]], axis=0)
            o = _bdot(p, vv)
            for half in range(2):
                scale = jnp.where(lane_left, inv_par[0][half], inv_par[1][half])
                attn_parts.append(o[half * BLOCK:(half + 1) * BLOCK] * scale)
        attn = jnp.concatenate(attn_parts, axis=1)
        mix_s[rows, 0:ATTN_WIDTH] = _rms_norm(attn, ng_ref[:, 0:ATTN_WIDTH]).astype(bf)

        vn = _layer_norm(gv_s[rows, :], lng_ref[...], lnb_ref[...])
        mixed_parts = []
        for pair in range(GMLP_GROUPS // 2):
            vp = vn[:, pair * LANES:(pair + 1) * LANES]
            zero = jnp.zeros_like(vp)
            rhs = jnp.concatenate([jnp.where(lane_left, vp, zero),
                                   jnp.where(lane_left, zero, vp)], axis=0).astype(bf)
            mixed_parts.append(_bdot(ws_ref[pair], rhs))
        mixed = jnp.concatenate(mixed_parts, axis=1) + bs_ref[...]
        sgu = gu_s[rows, :] * mixed
        mix_s[rows, ATTN_WIDTH:] = _rms_norm(sgu, ng_ref[:, ATTN_WIDTH:]).astype(bf)

    y = alpha * x_ref[0] + _bdot(mix_s[...], w_out_ref[...])
    o_ref[0] = _layer_norm(y, ln1g_ref[...], ln1b_ref[...])


def _ffn_kernel(x_ref, w1_ref, w2_ref, g_ref, b_ref, o_ref, *, alpha):
    bf = jnp.bfloat16
    for r in range(x_ref.shape[0] // FFN_ROWS):
        rows = pl.ds(r * FFN_ROWS, FFN_ROWS)
        x = x_ref[rows, :]
        h = jnp.maximum(_bdot(x.astype(bf), w1_ref[...]), 0.0)
        h = (h * h).astype(bf)
        y = alpha * x + _bdot(h, w2_ref[...])
        o_ref[rows, :] = _layer_norm(y, g_ref[...], b_ref[...])


def _const_spec(shape):
    zeros = (0,) * len(shape)
    return pl.BlockSpec(shape, lambda *_: zeros, pipeline_mode=pl.Buffered(1))


def _mix_call(x, sink, w_in, ln_g, ln_b, ws_cat, bs_full, norm_g, w_out, ln1_g, ln1_b, alpha):
    batch, seq, d = x.shape
    tile = MIX_TILE
    n_tiles = seq // tile
    bpt = tile // BLOCK
    n_seq_blocks = seq // BLOCK
    kern = functools.partial(_mix_kernel, alpha=alpha, n_tiles=n_tiles)
    in_specs = [
        pl.BlockSpec(memory_space=pltpu.SMEM),
        pl.BlockSpec((1, tile, d), lambda b, i: (b, i, 0)),
        pl.BlockSpec((1, BLOCK, d), lambda b, i: (b, jnp.maximum(i * bpt - 1, 0), 0)),
        pl.BlockSpec((1, BLOCK, d),
                     lambda b, i: (b, jnp.minimum((i + 1) * bpt, n_seq_blocks - 1), 0)),
        _const_spec(w_in.shape), _const_spec(ln_g.shape), _const_spec(ln_b.shape),
        _const_spec(ws_cat.shape), _const_spec(bs_full.shape), _const_spec(norm_g.shape),
        _const_spec(w_out.shape), _const_spec(ln1_g.shape), _const_spec(ln1_b.shape),
    ]
    bf = jnp.bfloat16
    scratch = [
        pltpu.VMEM((tile, ATTN_WIDTH), bf),
        pltpu.VMEM((4, tile + 2 * BLOCK, LANES), bf),
        pltpu.VMEM((4, tile + 2 * BLOCK, LANES), bf),
        pltpu.VMEM((tile, GMLP_WIDTH), jnp.float32),
        pltpu.VMEM((tile, GMLP_WIDTH), jnp.float32),
        pltpu.VMEM((tile, ATTN_WIDTH + GMLP_WIDTH), bf),
    ]
    return pl.pallas_call(
        kern,
        out_shape=jax.ShapeDtypeStruct(x.shape, x.dtype),
        grid=(batch, n_tiles),
        in_specs=in_specs,
        out_specs=pl.BlockSpec((1, tile, d), lambda b, i: (b, i, 0)),
        scratch_shapes=scratch,
        compiler_params=pltpu.CompilerParams(
            dimension_semantics=("arbitrary", "arbitrary"),
            vmem_limit_bytes=VMEM_LIMIT_BYTES),
        name="mix",
    )(sink, x, x, x, w_in, ln_g, ln_b, ws_cat, bs_full, norm_g, w_out, ln1_g, ln1_b)


def _ffn_call(x2d, w1, w2, g, b, alpha):
    n, d = x2d.shape
    kern = functools.partial(_ffn_kernel, alpha=alpha)
    return pl.pallas_call(
        kern,
        out_shape=jax.ShapeDtypeStruct(x2d.shape, x2d.dtype),
        grid=(n // FFN_TILE,),
        in_specs=[pl.BlockSpec((FFN_TILE, d), lambda i: (i, 0)),
                  _const_spec(w1.shape), _const_spec(w2.shape),
                  _const_spec(g.shape), _const_spec(b.shape)],
        out_specs=pl.BlockSpec((FFN_TILE, d), lambda i: (i, 0)),
        compiler_params=pltpu.CompilerParams(
            dimension_semantics=("arbitrary",),
            vmem_limit_bytes=VMEM_LIMIT_BYTES),
        name="ffn",
    )(x2d, w1, w2, g, b)


def kernel(x, w_in, sink, gmlp_ln_g, gmlp_ln_b, w_spatial, b_spatial, attn_norm_g, gmlp_norm_g,
           w_out, ln1_g, ln1_b, w_ff1, w_ff2, ln2_g, ln2_b):
    batch, seq, d = x.shape
    depth = w_in.shape[0]
    assert seq % MIX_TILE == 0 and (batch * seq) % FFN_TILE == 0
    assert w_in.shape[2] == IN_WIDTH and w_spatial.shape[1:] == (GMLP_GROUPS, BLOCK, BLOCK)
    alpha = (2.0 * depth) ** 0.25
    bf = jnp.bfloat16
    row = lambda a: a.reshape(1, -1)
    for l in range(depth):
        ws_cat = w_spatial[l].reshape(GMLP_GROUPS // 2, 2, BLOCK, BLOCK)
        ws_cat = ws_cat.transpose(0, 2, 1, 3).reshape(GMLP_GROUPS // 2, BLOCK, 2 * BLOCK).astype(bf)
        bs_full = jnp.repeat(b_spatial[l].T, GMLP_WIDTH // GMLP_GROUPS, axis=1)
        norm_g = jnp.concatenate([attn_norm_g[l], gmlp_norm_g[l]]).reshape(1, -1)
        x = _mix_call(x, sink[l], w_in[l].astype(bf), row(gmlp_ln_g[l]), row(gmlp_ln_b[l]),
                      ws_cat, bs_full, norm_g, w_out[l].astype(bf), row(ln1_g[l]), row(ln1_b[l]),
                      alpha)
        x = _ffn_call(x.reshape(batch * seq, d), w_ff1[l].astype(bf), w_ff2[l].astype(bf),
                      row(ln2_g[l]), row(ln2_b[l]), alpha).reshape(batch, seq, d)
    return x
```

```python
import functools

import jax
import jax.numpy as jnp
from jax import lax
from jax.experimental import pallas as pl
from jax.experimental.pallas import tpu as pltpu

HEAD_DIM = 64
N_Q_HEADS = 8
N_KV_HEADS = 2
Q_PER_KV = N_Q_HEADS // N_KV_HEADS
ATTN_WIDTH = N_Q_HEADS * HEAD_DIM
KV_WIDTH = N_KV_HEADS * HEAD_DIM
GMLP_WIDTH = 512
GMLP_GROUPS = 8
WINDOW = 128
BLOCK = 128
LN_EPS = 1e-5
NEG_INF = -1e30
LANES = 128

Q_OFF = 0
K_OFF = ATTN_WIDTH
V_OFF = ATTN_WIDTH + KV_WIDTH
GU_OFF = ATTN_WIDTH + 2 * KV_WIDTH
GV_OFF = GU_OFF + GMLP_WIDTH
IN_WIDTH = GV_OFF + GMLP_WIDTH

TILE = 512
FFN_ROWS = 256
FFN_CHUNK = 512
VMEM_LIMIT_BYTES = 56 * 1024 * 1024


def _bdot(a, b):
    return jnp.dot(a, b, preferred_element_type=jnp.float32)


def _gelu(x):
    return 0.5 * x * (1.0 + lax.erf(x * (0.5 ** 0.5)))


def _layer_norm(y, g, b):
    mu = jnp.mean(y, axis=-1, keepdims=True)
    yc = y - mu
    var = jnp.mean(yc * yc, axis=-1, keepdims=True)
    return yc * lax.rsqrt(var + LN_EPS) * g + b


def _rms_norm(y, g):
    ms = jnp.mean(y * y, axis=-1, keepdims=True)
    return y * lax.rsqrt(ms + LN_EPS) * g


def _half_masked(a, left):
    sw = pltpu.roll(a, HEAD_DIM, axis=1)
    zero = jnp.zeros_like(a)
    bf = jnp.bfloat16
    return (jnp.where(left, a, zero).astype(bf), jnp.where(left, zero, sw).astype(bf),
            jnp.where(left, sw, zero).astype(bf), jnp.where(left, zero, a).astype(bf))


def _mixer_outputs(seq_tile, n_tiles, sink_ref, x_ref, xp_ref, xn_ref, w_in_ref, lng_ref, lnb_ref,
                   ws_ref, bs_ref, ng_ref, q_s, k_s, v_s, gu_s, gv_s, mix_s, after_project,
                   after_scores):
    tile = x_ref.shape[1]
    n_blocks = tile // BLOCK
    bf = jnp.bfloat16

    xb = x_ref[0].astype(bf)
    q_s[...] = (_bdot(xb, w_in_ref[:, Q_OFF:K_OFF]) * (HEAD_DIM ** -0.5)).astype(bf)
    gu_s[...] = _gelu(_bdot(xb, w_in_ref[:, GU_OFF:GV_OFF]))
    gv_s[...] = _gelu(_bdot(xb, w_in_ref[:, GV_OFF:IN_WIDTH]))
    kv = _bdot(xb, w_in_ref[:, K_OFF:GU_OFF])
    xh = jnp.concatenate([xp_ref[0], xn_ref[0]], axis=0).astype(bf)
    kvh = _bdot(xh, w_in_ref[:, K_OFF:GU_OFF])

    def store_kv(start, kv_rows):
        n = kv_rows.shape[0]
        left = lax.broadcasted_iota(jnp.int32, (n, LANES), 1) < HEAD_DIM
        for i, a in enumerate(_half_masked(kv_rows[:, :KV_WIDTH], left)):
            k_s[i, start:start + n, :] = a
        for i, a in enumerate(_half_masked(kv_rows[:, KV_WIDTH:], left)):
            v_s[i, start:start + n, :] = a

    store_kv(0, kvh[:BLOCK])
    store_kv(BLOCK, kv)
    store_kv(BLOCK + tile, kvh[BLOCK:])
    after_project()

    row = lax.broadcasted_iota(jnp.int32, (BLOCK, 3 * BLOCK), 0)
    col = lax.broadcasted_iota(jnp.int32, (BLOCK, 3 * BLOCK), 1)
    dist = jnp.abs(col - BLOCK - row)
    distf = dist.astype(jnp.float32)
    in_window = dist <= WINDOW
    lane_left = lax.broadcasted_iota(jnp.int32, (BLOCK, LANES), 1) < HEAD_DIM

    unit = 0
    for j in range(n_blocks):
        r0, r1 = j * BLOCK, (j + 1) * BLOCK
        k0, k1 = j * BLOCK, (j + 3) * BLOCK
        valid = in_window
        if j == 0:
            valid = valid & (col >= jnp.where(seq_tile == 0, BLOCK, 0))
        if j == n_blocks - 1:
            valid = valid & (col < jnp.where(seq_tile == n_tiles - 1, 2 * BLOCK, 3 * BLOCK))

        attn_parts = []
        for g in range(N_KV_HEADS):
            lhs = jnp.concatenate([q_s[r0:r1, g * 256:g * 256 + LANES],
                                   q_s[r0:r1, g * 256 + LANES:(g + 1) * 256]], axis=0)
            e_par, inv_par = [], []
            for par in range(2):
                s = lax.dot_general(lhs, k_s[2 * g + par, k0:k1, :],
                                    (((1,), (1,)), ((), ())),
                                    preferred_element_type=jnp.float32)
                after_scores(unit)
                unit += 1
                e_half, inv_half = [], []
                for half in range(2):
                    h = Q_PER_KV * g + 2 * half + par
                    slope = 2.0 ** (-8.0 * (h + 1) / N_Q_HEADS)
                    sink = sink_ref[h]
                    sh = s[half * BLOCK:(half + 1) * BLOCK]
                    sb = jnp.where(valid, sh - slope * distf, NEG_INF)
                    m = jnp.maximum(jnp.max(sb, axis=-1, keepdims=True), sink)
                    e = jnp.exp(sb - m)
                    den = jnp.sum(e, axis=-1, keepdims=True) + jnp.exp(sink - m)
                    e_half.append(e.astype(bf))
                    inv_half.append(1.0 / den)
                e_par.append(jnp.concatenate(e_half, axis=0))
                inv_par.append(inv_half)
            p = jnp.concatenate(e_par, axis=1)
            vv = jnp.concatenate([v_s[2 * g, k0:k1, :], v_s[2 * g + 1, k0:k1, :]], axis=0)
            o = _bdot(p, vv)
            for half in range(2):
                scale = jnp.where(lane_left, inv_par[0][half], inv_par[1][half])
                attn_parts.append(o[half * BLOCK:(half + 1) * BLOCK] * scale)
        attn = jnp.concatenate(attn_parts, axis=1)
        mix_s[r0:r1, 0:ATTN_WIDTH] = _rms_norm(attn, ng_ref[:, 0:ATTN_WIDTH]).astype(bf)

        vn = _layer_norm(gv_s[r0:r1, :], lng_ref[...], lnb_ref[...])
        mixed_parts = []
        for pair in range(GMLP_GROUPS // 2):
            vp = vn[:, pair * LANES:(pair + 1) * LANES]
            zero = jnp.zeros_like(vp)
            rhs = jnp.concatenate([jnp.where(lane_left, vp, zero),
                                   jnp.where(lane_left, zero, vp)], axis=0).astype(bf)
            mixed_parts.append(_bdot(ws_ref[pair], rhs))
        mixed = jnp.concatenate(mixed_parts, axis=1) + bs_ref[...]
        sgu = gu_s[r0:r1, :] * mixed
        mix_s[r0:r1, ATTN_WIDTH:] = _rms_norm(sgu, ng_ref[:, ATTN_WIDTH:]).astype(bf)


def _channel_mixing_piece(piece, alpha, x1_ref, x1b_ref, w1_ref, w2_ref, g_ref, b_ref, acc_s,
                          o_ref):
    bf = jnp.bfloat16
    n_chunks = w1_ref.shape[1] // FFN_CHUNK
    r, c = divmod(piece, n_chunks)
    r0, r1 = r * FFN_ROWS, (r + 1) * FFN_ROWS
    c0, c1 = c * FFN_CHUNK, (c + 1) * FFN_CHUNK
    h = jnp.maximum(_bdot(x1b_ref[r0:r1, :], w1_ref[:, c0:c1]), 0.0)
    part = _bdot((h * h).astype(bf), w2_ref[c0:c1, :])
    if c == 0:
        acc_s[r0:r1, :] = alpha * x1_ref[r0:r1, :] + part
    elif c < n_chunks - 1:
        acc_s[r0:r1, :] += part
    else:
        o_ref[0, r0:r1, :] = _layer_norm(acc_s[r0:r1, :] + part, g_ref[...], b_ref[...])


def _block_kernel(sink_ref, x_ref, xp_ref, xn_ref, w_in_ref, lng_ref, lnb_ref, ws_ref, bs_ref,
                  ng_ref, w_out_ref, ln1g_ref, ln1b_ref, w1_ref, w2_ref, ln2g_ref, ln2b_ref,
                  o_ref, q_s, k_s, v_s, gu_s, gv_s, mix_s, xres_s, x1_s, x1b_s, acc_s, *, alpha,
                  n_tiles, n_steps):
    step = pl.program_id(0)

    @pl.when(step == 0)
    def _():
        mix_s[...] = jnp.zeros(mix_s.shape, mix_s.dtype)
        xres_s[...] = jnp.zeros(xres_s.shape, xres_s.dtype)

    seq_tile = jnp.minimum(step, n_steps - 2) % n_tiles
    n_units = (x_ref.shape[1] // BLOCK) * N_KV_HEADS * 2
    n_pieces = (x_ref.shape[1] // FFN_ROWS) * (w1_ref.shape[1] // FFN_CHUNK)
    per_unit = n_pieces // n_units
    assert per_unit * n_units == n_pieces

    mix_out = _bdot(mix_s[...], w_out_ref[...])

    def finish_previous_tile():
        x1 = _layer_norm(xres_s[...] + mix_out, ln1g_ref[...], ln1b_ref[...])
        x1_s[...] = x1
        x1b_s[...] = x1.astype(jnp.bfloat16)
        xres_s[...] = alpha * x_ref[0]

    def channel_mixing_pieces(u):
        for piece in range(u * per_unit, (u + 1) * per_unit):
            _channel_mixing_piece(piece, alpha, x1_s, x1b_s, w1_ref, w2_ref, ln2g_ref, ln2b_ref,
                                  acc_s, o_ref)

    _mixer_outputs(seq_tile, n_tiles, sink_ref, x_ref, xp_ref, xn_ref, w_in_ref, lng_ref, lnb_ref,
                   ws_ref, bs_ref, ng_ref, q_s, k_s, v_s, gu_s, gv_s, mix_s,
                   finish_previous_tile, channel_mixing_pieces)


def _const_spec(shape):
    zeros = (0,) * len(shape)
    return pl.BlockSpec(shape, lambda *_: zeros, pipeline_mode=pl.Buffered(1))


def _block_call(x, sink, consts, alpha):
    batch, seq, d = x.shape
    n_tiles = seq // TILE
    n_steps = batch * n_tiles + 1
    bpt = TILE // BLOCK
    n_seq_blocks = seq // BLOCK

    def mix_tile(s):
        t = jnp.minimum(s, n_steps - 2)
        return t // n_tiles, t % n_tiles

    def x_map(s):
        b, i = mix_tile(s)
        return b, i, 0

    def prev_map(s):
        b, i = mix_tile(s)
        return b, jnp.maximum(i * bpt - 1, 0), 0

    def next_map(s):
        b, i = mix_tile(s)
        return b, jnp.minimum((i + 1) * bpt, n_seq_blocks - 1), 0

    def out_map(s):
        t = jnp.maximum(s - 1, 0)
        return t // n_tiles, t % n_tiles, 0

    kern = functools.partial(_block_kernel, alpha=alpha, n_tiles=n_tiles, n_steps=n_steps)
    in_specs = [
        pl.BlockSpec(memory_space=pltpu.SMEM),
        pl.BlockSpec((1, TILE, d), x_map),
        pl.BlockSpec((1, BLOCK, d), prev_map),
        pl.BlockSpec((1, BLOCK, d), next_map),
    ] + [_const_spec(c.shape) for c in consts]
    bf = jnp.bfloat16
    scratch = [
        pltpu.VMEM((TILE, ATTN_WIDTH), bf),
        pltpu.VMEM((4, TILE + 2 * BLOCK, LANES), bf),
        pltpu.VMEM((4, TILE + 2 * BLOCK, LANES), bf),
        pltpu.VMEM((TILE, GMLP_WIDTH), jnp.float32),
        pltpu.VMEM((TILE, GMLP_WIDTH), jnp.float32),
        pltpu.VMEM((TILE, ATTN_WIDTH + GMLP_WIDTH), bf),
        pltpu.VMEM((TILE, d), jnp.float32),
        pltpu.VMEM((TILE, d), jnp.float32),
        pltpu.VMEM((TILE, d), bf),
        pltpu.VMEM((TILE, d), jnp.float32),
    ]
    return pl.pallas_call(
        kern,
        out_shape=jax.ShapeDtypeStruct(x.shape, x.dtype),
        grid=(n_steps,),
        in_specs=in_specs,
        out_specs=pl.BlockSpec((1, TILE, d), out_map),
        scratch_shapes=scratch,
        compiler_params=pltpu.CompilerParams(
            dimension_semantics=("arbitrary",),
            vmem_limit_bytes=VMEM_LIMIT_BYTES),
        name="block",
    )(sink, x, x, x, *consts)


def kernel(x, w_in, sink, gmlp_ln_g, gmlp_ln_b, w_spatial, b_spatial, attn_norm_g, gmlp_norm_g,
           w_out, ln1_g, ln1_b, w_ff1, w_ff2, ln2_g, ln2_b):
    batch, seq, d = x.shape
    depth = w_in.shape[0]
    assert seq % TILE == 0
    assert w_in.shape[2] == IN_WIDTH and w_spatial.shape[1:] == (GMLP_GROUPS, BLOCK, BLOCK)
    alpha = (2.0 * depth) ** 0.25
    bf = jnp.bfloat16
    row = lambda a: a.reshape(1, -1)
    for l in range(depth):
        ws_cat = w_spatial[l].reshape(GMLP_GROUPS // 2, 2, BLOCK, BLOCK)
        ws_cat = ws_cat.transpose(0, 2, 1, 3).reshape(GMLP_GROUPS // 2, BLOCK, 2 * BLOCK).astype(bf)
        bs_full = jnp.repeat(b_spatial[l].T, GMLP_WIDTH // GMLP_GROUPS, axis=1)
        norm_g = jnp.concatenate([attn_norm_g[l], gmlp_norm_g[l]]).reshape(1, -1)
        consts = (w_in[l].astype(bf), row(gmlp_ln_g[l]), row(gmlp_ln_b[l]), ws_cat, bs_full,
                  norm_g, w_out[l].astype(bf), row(ln1_g[l]), row(ln1_b[l]),
                  w_ff1[l].astype(bf), w_ff2[l].astype(bf), row(ln2_g[l]), row(ln2_b[l]))
        x = _block_call(x, sink[l], consts, alpha)
    return x
```

```python
import functools

import jax
import jax.numpy as jnp
from jax import lax
from jax.experimental import pallas as pl
from jax.experimental.pallas import tpu as pltpu

HEAD_DIM = 64
N_Q_HEADS = 8
N_KV_HEADS = 2
Q_PER_KV = N_Q_HEADS // N_KV_HEADS
ATTN_WIDTH = N_Q_HEADS * HEAD_DIM
KV_WIDTH = N_KV_HEADS * HEAD_DIM
GMLP_WIDTH = 512
GMLP_GROUPS = 8
WINDOW = 128
BLOCK = 128
LN_EPS = 1e-5
NEG_INF = -1e30
LANES = 128

Q_OFF = 0
K_OFF = ATTN_WIDTH
V_OFF = ATTN_WIDTH + KV_WIDTH
GU_OFF = ATTN_WIDTH + 2 * KV_WIDTH
GV_OFF = GU_OFF + GMLP_WIDTH
IN_WIDTH = GV_OFF + GMLP_WIDTH

TILE = 512
FFN_ROWS = 256
FFN_CHUNK = 512
PIPELINE_DEPTH = 3
VMEM_LIMIT_BYTES = 60 * 1024 * 1024


def _bdot(a, b):
    return jnp.dot(a, b, preferred_element_type=jnp.float32)


def _gelu(x):
    return 0.5 * x * (1.0 + lax.erf(x * (0.5 ** 0.5)))


def _layer_norm(y, g, b):
    mu = jnp.mean(y, axis=-1, keepdims=True)
    yc = y - mu
    var = jnp.mean(yc * yc, axis=-1, keepdims=True)
    return yc * lax.rsqrt(var + LN_EPS) * g + b


def _rms_norm(y, g):
    ms = jnp.mean(y * y, axis=-1, keepdims=True)
    return y * lax.rsqrt(ms + LN_EPS) * g


def _half_masked(a, left):
    sw = pltpu.roll(a, HEAD_DIM, axis=1)
    zero = jnp.zeros_like(a)
    bf = jnp.bfloat16
    return (jnp.where(left, a, zero).astype(bf), jnp.where(left, zero, sw).astype(bf),
            jnp.where(left, sw, zero).astype(bf), jnp.where(left, zero, a).astype(bf))


def _project(x_ref, xp_ref, xn_ref, w_in_ref, q_s, k_s, v_s, gu_s, gv_s):
    tile = x_ref.shape[1]
    bf = jnp.bfloat16
    xb = x_ref[0].astype(bf)
    gu_s[...] = _gelu(_bdot(xb, w_in_ref[:, GU_OFF:GV_OFF]))
    gv_s[...] = _gelu(_bdot(xb, w_in_ref[:, GV_OFF:IN_WIDTH]))
    kv = _bdot(xb, w_in_ref[:, K_OFF:GU_OFF])
    xh = jnp.concatenate([xp_ref[0], xn_ref[0]], axis=0).astype(bf)
    kvh = _bdot(xh, w_in_ref[:, K_OFF:GU_OFF])
    q_s[...] = (_bdot(xb, w_in_ref[:, Q_OFF:K_OFF]) * (HEAD_DIM ** -0.5)).astype(bf)

    def store_kv(start, kv_rows):
        n = kv_rows.shape[0]
        left = lax.broadcasted_iota(jnp.int32, (n, LANES), 1) < HEAD_DIM
        for i, a in enumerate(_half_masked(kv_rows[:, :KV_WIDTH], left)):
            k_s[i, start:start + n, :] = a
        for i, a in enumerate(_half_masked(kv_rows[:, KV_WIDTH:], left)):
            v_s[i, start:start + n, :] = a

    store_kv(0, kvh[:BLOCK])
    store_kv(BLOCK, kv)
    store_kv(BLOCK + tile, kvh[BLOCK:])


def _mixer_outputs(seq_tile, n_tiles, sink_ref, lng_ref, lnb_ref, ws_ref, bs_ref, ng_ref,
                   q_s, k_s, v_s, gu_s, gv_s, mix_s, after_scores):
    n_blocks = q_s.shape[0] // BLOCK
    bf = jnp.bfloat16
    row = lax.broadcasted_iota(jnp.int32, (BLOCK, 3 * BLOCK), 0)
    col = lax.broadcasted_iota(jnp.int32, (BLOCK, 3 * BLOCK), 1)
    dist = jnp.abs(col - BLOCK - row)
    distf = dist.astype(jnp.float32)
    in_window = dist <= WINDOW
    lane_left = lax.broadcasted_iota(jnp.int32, (BLOCK, LANES), 1) < HEAD_DIM

    unit = 0
    for j in range(n_blocks):
        r0, r1 = j * BLOCK, (j + 1) * BLOCK
        k0, k1 = j * BLOCK, (j + 3) * BLOCK
        valid = in_window
        if j == 0:
            valid = valid & (col >= jnp.where(seq_tile == 0, BLOCK, 0))
        if j == n_blocks - 1:
            valid = valid & (col < jnp.where(seq_tile == n_tiles - 1, 2 * BLOCK, 3 * BLOCK))

        attn_parts = []
        for g in range(N_KV_HEADS):
            lhs = jnp.concatenate([q_s[r0:r1, g * 256:g * 256 + LANES],
                                   q_s[r0:r1, g * 256 + LANES:(g + 1) * 256]], axis=0)
            p_par = []
            for par in range(2):
                s = lax.dot_general(lhs, k_s[2 * g + par, k0:k1, :],
                                    (((1,), (1,)), ((), ())),
                                    preferred_element_type=jnp.float32)
                after_scores(unit)
                unit += 1
                p_half = []
                for half in range(2):
                    h = Q_PER_KV * g + 2 * half + par
                    slope = 2.0 ** (-8.0 * (h + 1) / N_Q_HEADS)
                    sink = sink_ref[h]
                    sh = s[half * BLOCK:(half + 1) * BLOCK]
                    sb = jnp.where(valid, sh - slope * distf, NEG_INF)
                    m = jnp.maximum(jnp.max(sb, axis=-1, keepdims=True), sink)
                    e = jnp.exp(sb - m)
                    den = jnp.sum(e, axis=-1, keepdims=True) + jnp.exp(sink - m)
                    p_half.append((e * (1.0 / den)).astype(bf))
                p_par.append(jnp.concatenate(p_half, axis=0))
            p = jnp.concatenate(p_par, axis=1)
            vv = jnp.concatenate([v_s[2 * g, k0:k1, :], v_s[2 * g + 1, k0:k1, :]], axis=0)
            o = _bdot(p, vv)
            attn_parts += [o[:BLOCK], o[BLOCK:]]
        attn = jnp.concatenate(attn_parts, axis=1)
        mix_s[r0:r1, 0:ATTN_WIDTH] = _rms_norm(attn, ng_ref[:, 0:ATTN_WIDTH]).astype(bf)

        vn = _layer_norm(gv_s[r0:r1, :], lng_ref[...], lnb_ref[...])
        mixed_parts = []
        for pair in range(GMLP_GROUPS // 2):
            vp = vn[:, pair * LANES:(pair + 1) * LANES]
            zero = jnp.zeros_like(vp)
            rhs = jnp.concatenate([jnp.where(lane_left, vp, zero),
                                   jnp.where(lane_left, zero, vp)], axis=0).astype(bf)
            mixed_parts.append(_bdot(ws_ref[pair], rhs))
        mixed = jnp.concatenate(mixed_parts, axis=1) + bs_ref[...]
        sgu = gu_s[r0:r1, :] * mixed
        mix_s[r0:r1, ATTN_WIDTH:] = _rms_norm(sgu, ng_ref[:, ATTN_WIDTH:]).astype(bf)


def _piece_rows_cols(piece, w1_ref):
    r, c = divmod(piece, w1_ref.shape[1] // FFN_CHUNK)
    return r * FFN_ROWS, (r + 1) * FFN_ROWS, c * FFN_CHUNK, (c + 1) * FFN_CHUNK


def _mlp_piece_hidden(piece, x1b_ref, w1_ref):
    r0, r1, c0, c1 = _piece_rows_cols(piece, w1_ref)
    h = jnp.maximum(_bdot(x1b_ref[r0:r1, :], w1_ref[:, c0:c1]), 0.0)
    return (h * h).astype(jnp.bfloat16)


def _mlp_piece_output(piece, hidden, alpha, x1_ref, w1_ref, w2_ref, g_ref, b_ref, acc_s, o_ref):
    r0, r1, c0, c1 = _piece_rows_cols(piece, w1_ref)
    part = _bdot(hidden, w2_ref[c0:c1, :])
    if c0 == 0:
        acc_s[r0:r1, :] = alpha * x1_ref[r0:r1, :] + part
    elif c1 < w1_ref.shape[1]:
        acc_s[r0:r1, :] += part
    else:
        o_ref[0, r0:r1, :] = _layer_norm(acc_s[r0:r1, :] + part, g_ref[...], b_ref[...])


def _block_kernel(sink_ref, x_ref, xp_ref, xn_ref, xres_ref, w_in_ref, lng_ref, lnb_ref, ws_ref,
                  bs_ref, ng_ref, w_out_ref, ln1g_ref, ln1b_ref, w1_ref, w2_ref, ln2g_ref,
                  ln2b_ref, o_ref, q_s, k_s, v_s, gu_s, gv_s, mix_s, x1_s, x1b_s, acc_s, *, alpha,
                  n_tiles, n_steps):
    step = pl.program_id(0)
    last_tile = n_steps - PIPELINE_DEPTH

    @pl.when(step == 0)
    def _():
        for ref in (q_s, k_s, v_s, gu_s, gv_s, mix_s):
            ref[...] = jnp.zeros(ref.shape, ref.dtype)

    for r0 in range(0, x1_s.shape[0], BLOCK):
        r1 = r0 + BLOCK
        y = alpha * xres_ref[0, r0:r1, :] + _bdot(mix_s[r0:r1, :], w_out_ref[...])
        x1 = _layer_norm(y, ln1g_ref[...], ln1b_ref[...])
        x1_s[r0:r1, :] = x1
        x1b_s[r0:r1, :] = x1.astype(jnp.bfloat16)

    n_units = (x1_s.shape[0] // BLOCK) * N_KV_HEADS * 2
    n_pieces = (x1_s.shape[0] // FFN_ROWS) * (w1_ref.shape[1] // FFN_CHUNK)
    assert n_pieces == n_units
    hidden = {}

    def mlp_stage(u):
        if u < n_pieces:
            hidden[u] = _mlp_piece_hidden(u, x1b_s, w1_ref)
        if u > 0:
            _mlp_piece_output(u - 1, hidden.pop(u - 1), alpha, x1_s, w1_ref, w2_ref, ln2g_ref,
                              ln2b_ref, acc_s, o_ref)

    seq_tile = jnp.clip(step - 1, 0, last_tile) % n_tiles
    _mixer_outputs(seq_tile, n_tiles, sink_ref, lng_ref, lnb_ref, ws_ref, bs_ref, ng_ref,
                   q_s, k_s, v_s, gu_s, gv_s, mix_s, mlp_stage)
    mlp_stage(n_pieces)

    _project(x_ref, xp_ref, xn_ref, w_in_ref, q_s, k_s, v_s, gu_s, gv_s)


def _const_spec(shape):
    zeros = (0,) * len(shape)
    return pl.BlockSpec(shape, lambda *_: zeros, pipeline_mode=pl.Buffered(1))


def _block_call(x, sink, consts, alpha):
    batch, seq, d = x.shape
    n_tiles = seq // TILE
    n_steps = batch * n_tiles + PIPELINE_DEPTH - 1
    last_tile = batch * n_tiles - 1
    bpt = TILE // BLOCK
    n_seq_blocks = seq // BLOCK

    def tile_of(s, lag):
        t = jnp.clip(s - lag, 0, last_tile)
        return t // n_tiles, t % n_tiles

    def x_map(s):
        b, i = tile_of(s, 0)
        return b, i, 0

    def prev_map(s):
        b, i = tile_of(s, 0)
        return b, jnp.maximum(i * bpt - 1, 0), 0

    def next_map(s):
        b, i = tile_of(s, 0)
        return b, jnp.minimum((i + 1) * bpt, n_seq_blocks - 1), 0

    def late_map(s):
        b, i = tile_of(s, PIPELINE_DEPTH - 1)
        return b, i, 0

    kern = functools.partial(_block_kernel, alpha=alpha, n_tiles=n_tiles, n_steps=n_steps)
    in_specs = [
        pl.BlockSpec(memory_space=pltpu.SMEM),
        pl.BlockSpec((1, TILE, d), x_map),
        pl.BlockSpec((1, BLOCK, d), prev_map),
        pl.BlockSpec((1, BLOCK, d), next_map),
        pl.BlockSpec((1, TILE, d), late_map),
    ] + [_const_spec(c.shape) for c in consts]
    bf = jnp.bfloat16
    scratch = [
        pltpu.VMEM((TILE, ATTN_WIDTH), bf),
        pltpu.VMEM((4, TILE + 2 * BLOCK, LANES), bf),
        pltpu.VMEM((4, TILE + 2 * BLOCK, LANES), bf),
        pltpu.VMEM((TILE, GMLP_WIDTH), jnp.float32),
        pltpu.VMEM((TILE, GMLP_WIDTH), jnp.float32),
        pltpu.VMEM((TILE, ATTN_WIDTH + GMLP_WIDTH), bf),
        pltpu.VMEM((TILE, d), jnp.float32),
        pltpu.VMEM((TILE, d), bf),
        pltpu.VMEM((TILE, d), jnp.float32),
    ]
    return pl.pallas_call(
        kern,
        out_shape=jax.ShapeDtypeStruct(x.shape, x.dtype),
        grid=(n_steps,),
        in_specs=in_specs,
        out_specs=pl.BlockSpec((1, TILE, d), late_map),
        scratch_shapes=scratch,
        compiler_params=pltpu.CompilerParams(
            dimension_semantics=("arbitrary",),
            vmem_limit_bytes=VMEM_LIMIT_BYTES),
        name="block",
    )(sink, x, x, x, x, *consts)


def kernel(x, w_in, sink, gmlp_ln_g, gmlp_ln_b, w_spatial, b_spatial, attn_norm_g, gmlp_norm_g,
           w_out, ln1_g, ln1_b, w_ff1, w_ff2, ln2_g, ln2_b):
    batch, seq, d = x.shape
    depth = w_in.shape[0]
    assert seq % TILE == 0
    assert w_in.shape[2] == IN_WIDTH and w_spatial.shape[1:] == (GMLP_GROUPS, BLOCK, BLOCK)
    alpha = (2.0 * depth) ** 0.25
    bf = jnp.bfloat16
    row = lambda a: a.reshape(1, -1)
    for l in range(depth):
        ws_cat = w_spatial[l].reshape(GMLP_GROUPS // 2, 2, BLOCK, BLOCK)
        ws_cat = ws_cat.transpose(0, 2, 1, 3).reshape(GMLP_GROUPS // 2, BLOCK, 2 * BLOCK).astype(bf)
        bs_full = jnp.repeat(b_spatial[l].T, GMLP_WIDTH // GMLP_GROUPS, axis=1)
        norm_g = jnp.concatenate([attn_norm_g[l], gmlp_norm_g[l]]).reshape(1, -1)
        consts = (w_in[l].astype(bf), row(gmlp_ln_g[l]), row(gmlp_ln_b[l]), ws_cat, bs_full,
                  norm_g, w_out[l].astype(bf), row(ln1_g[l]), row(ln1_b[l]),
                  w_ff1[l].astype(bf), w_ff2[l].astype(bf), row(ln2_g[l]), row(ln2_b[l]))
        x = _block_call(x, sink[l], consts, alpha)
    return x
```

```python
import functools
import math

import jax
import jax.numpy as jnp
from jax import lax
from jax.experimental import pallas as pl
from jax.experimental.pallas import tpu as pltpu

HEAD_DIM = 64
N_Q_HEADS = 8
N_KV_HEADS = 2
Q_PER_KV = N_Q_HEADS // N_KV_HEADS
ATTN_WIDTH = N_Q_HEADS * HEAD_DIM
KV_WIDTH = N_KV_HEADS * HEAD_DIM
GMLP_WIDTH = 512
GMLP_GROUPS = 8
WINDOW = 128
BLOCK = 128
LN_EPS = 1e-5
NEG_INF = -1e30
LOG2E = math.log2(math.e)
LANES = 128

Q_OFF = 0
K_OFF = ATTN_WIDTH
V_OFF = ATTN_WIDTH + KV_WIDTH
GU_OFF = ATTN_WIDTH + 2 * KV_WIDTH
GV_OFF = GU_OFF + GMLP_WIDTH
IN_WIDTH = GV_OFF + GMLP_WIDTH

TILE = 512
FFN_CHUNK = 512
PIPELINE_DEPTH = 3
VMEM_LIMIT_BYTES = 60 * 1024 * 1024


def _bdot(a, b):
    return jnp.dot(a, b, preferred_element_type=jnp.float32)


def _gelu(x):
    return 0.5 * x * (1.0 + lax.erf(x * (0.5 ** 0.5)))


def _layer_norm(y, g, b):
    mu = jnp.mean(y, axis=-1, keepdims=True)
    yc = y - mu
    var = jnp.mean(yc * yc, axis=-1, keepdims=True)
    return yc * lax.rsqrt(var + LN_EPS) * g + b


def _rms_norm(y, g):
    ms = jnp.mean(y * y, axis=-1, keepdims=True)
    return y * lax.rsqrt(ms + LN_EPS) * g


def _half_masked(a, left):
    sw = pltpu.roll(a, HEAD_DIM, axis=1)
    zero = jnp.zeros_like(a)
    bf = jnp.bfloat16
    return (jnp.where(left, a, zero).astype(bf), jnp.where(left, zero, sw).astype(bf),
            jnp.where(left, sw, zero).astype(bf), jnp.where(left, zero, a).astype(bf))


def _fill_bias_table(bias_s):
    row = lax.broadcasted_iota(jnp.int32, (BLOCK, 3 * BLOCK), 0)
    col = lax.broadcasted_iota(jnp.int32, (BLOCK, 3 * BLOCK), 1)
    dist = jnp.abs(col - BLOCK - row)
    distf = dist.astype(jnp.float32)
    for h in range(N_Q_HEADS):
        slope = 2.0 ** (-8.0 * (h + 1) / N_Q_HEADS)
        bias_s[h] = jnp.where(dist <= WINDOW, (-slope * LOG2E) * distf, NEG_INF)


def _store_kv(k_s, v_s, start, kv_rows):
    n = kv_rows.shape[0]
    left = lax.broadcasted_iota(jnp.int32, (n, LANES), 1) < HEAD_DIM
    for i, a in enumerate(_half_masked(kv_rows[:, :KV_WIDTH], left)):
        k_s[i, start:start + n, :] = a
    for i, a in enumerate(_half_masked(kv_rows[:, KV_WIDTH:], left)):
        v_s[i, start:start + n, :] = a


def _mixer_outputs(seq_tile, n_tiles, sink_ref, lng_ref, lnb_ref, ws_ref, bs_ref, ng_ref,
                   q_s, k_s, v_s, gu_s, gv_s, bias_s, mix_s, fill):
    n_blocks = q_s.shape[0] // BLOCK
    bf = jnp.bfloat16
    col = lax.broadcasted_iota(jnp.int32, (BLOCK, BLOCK), 1)
    lane_left = col < HEAD_DIM
    first_valid = jnp.where(seq_tile == 0, BLOCK, 0)
    last_valid = jnp.where(seq_tile == n_tiles - 1, 0, BLOCK)

    for j in range(n_blocks):
        r0, r1 = j * BLOCK, (j + 1) * BLOCK
        k0, k1 = j * BLOCK, (j + 3) * BLOCK
        attn_parts = []
        for g in range(N_KV_HEADS):
            lhs = jnp.concatenate([q_s[r0:r1, g * 256:g * 256 + LANES],
                                   q_s[r0:r1, g * 256 + LANES:(g + 1) * 256]], axis=0)
            p_par = []
            for par in range(2):
                s = lax.dot_general(lhs, k_s[2 * g + par, k0:k1, :],
                                    (((1,), (1,)), ((), ())),
                                    preferred_element_type=jnp.float32)
                fill()
                p_half = []
                for half in range(2):
                    h = Q_PER_KV * g + 2 * half + par
                    sink = sink_ref[h] * LOG2E
                    sb = s[half * BLOCK:(half + 1) * BLOCK] + bias_s[h]
                    if j == 0:
                        sb = jnp.concatenate(
                            [jnp.where(col >= first_valid, sb[:, :BLOCK], NEG_INF),
                             sb[:, BLOCK:]], axis=1)
                    if j == n_blocks - 1:
                        sb = jnp.concatenate(
                            [sb[:, :2 * BLOCK],
                             jnp.where(col < last_valid, sb[:, 2 * BLOCK:], NEG_INF)], axis=1)
                    m = jnp.maximum(jnp.max(sb, axis=-1, keepdims=True), sink)
                    e = jnp.exp2(sb - m)
                    den = jnp.sum(e, axis=-1, keepdims=True) + jnp.exp2(sink - m)
                    p_half.append((e * (1.0 / den)).astype(bf))
                p_par.append(jnp.concatenate(p_half, axis=0))
            p = jnp.concatenate(p_par, axis=1)
            vv = jnp.concatenate([v_s[2 * g, k0:k1, :], v_s[2 * g + 1, k0:k1, :]], axis=0)
            o = _bdot(p, vv)
            fill()
            attn_parts += [o[:BLOCK], o[BLOCK:]]
        attn = jnp.concatenate(attn_parts, axis=1)
        mix_s[r0:r1, 0:ATTN_WIDTH] = _rms_norm(attn, ng_ref[:, 0:ATTN_WIDTH]).astype(bf)

        vn = _layer_norm(_gelu(gv_s[r0:r1, :]), lng_ref[...], lnb_ref[...])
        mixed_parts = []
        for pair in range(GMLP_GROUPS // 2):
            vp = vn[:, pair * LANES:(pair + 1) * LANES]
            zero = jnp.zeros_like(vp)
            rhs = jnp.concatenate([jnp.where(lane_left, vp, zero),
                                   jnp.where(lane_left, zero, vp)], axis=0).astype(bf)
            mixed_parts.append(_bdot(ws_ref[pair], rhs))
        fill()
        mixed = jnp.concatenate(mixed_parts, axis=1) + bs_ref[...]
        sgu = _gelu(gu_s[r0:r1, :]) * mixed
        mix_s[r0:r1, ATTN_WIDTH:] = _rms_norm(sgu, ng_ref[:, ATTN_WIDTH:]).astype(bf)


HOOKS_PER_BLOCK = 7

FILL_PLAN = (
    "h0", "h1", None, "o0", "h2", None, "o1",
    "h3", "w0", None, "o2", "h4", None, "o3",
    "h5", "w1", None, "o4", "h6", None, "o5",
    "h7", "w2", None, "o6", None, None, "o7",
)


def _block_kernel(sink_ref, x_ref, xp_ref, xn_ref, w_in_ref, lng_ref, lnb_ref, ws_ref, bs_ref,
                  ng_ref, w_out_ref, ln1g_ref, ln1b_ref, w1_ref, w2_ref, ln2g_ref, ln2b_ref,
                  o_ref, q_s, k_s, v_s, gu_s, gv_s, bias_s, mix_s, y_s, x1_s, x1b_s, acc_s,
                  *, alpha, n_tiles, n_steps):
    step = pl.program_id(0)
    last_tile = n_steps - PIPELINE_DEPTH
    tile = x_ref.shape[1]
    bf = jnp.bfloat16
    assert len(FILL_PLAN) == HOOKS_PER_BLOCK * (tile // BLOCK)
    assert w1_ref.shape[1] == 8 * FFN_CHUNK and tile == 4 * BLOCK

    @pl.when(step == 0)
    def _():
        _fill_bias_table(bias_s)
        y_s[...] = jnp.zeros(y_s.shape, y_s.dtype)
        acc_s[...] = jnp.zeros(acc_s.shape, acc_s.dtype)

    o_ref[0] = _layer_norm(acc_s[...], ln2g_ref[...], ln2b_ref[...])
    x1 = _layer_norm(y_s[...], ln1g_ref[...], ln1b_ref[...])
    x1_s[...] = x1
    x1b_s[...] = x1.astype(bf)

    xb = x_ref[0].astype(bf)
    xh = jnp.concatenate([xp_ref[0], xn_ref[0]], axis=0).astype(bf)
    kv = _bdot(xb, w_in_ref[:, K_OFF:GU_OFF])
    kvh = _bdot(xh, w_in_ref[:, K_OFF:GU_OFF])
    q_s[...] = (_bdot(xb, w_in_ref[:, Q_OFF:K_OFF]) * (LOG2E * HEAD_DIM ** -0.5)).astype(bf)
    _store_kv(k_s, v_s, 0, kvh[:BLOCK])
    _store_kv(k_s, v_s, BLOCK, kv)
    _store_kv(k_s, v_s, BLOCK + tile, kvh[BLOCK:])
    gu_s[...] = _bdot(xb, w_in_ref[:, GU_OFF:GV_OFF])
    gv_s[...] = _bdot(xb, w_in_ref[:, GV_OFF:IN_WIDTH])

    hidden = {}

    def mlp_hidden(c):
        h = jnp.maximum(_bdot(x1b_s[...], w1_ref[:, c * FFN_CHUNK:(c + 1) * FFN_CHUNK]), 0.0)
        hidden[c] = (h * h).astype(bf)

    def mlp_output(c):
        part = _bdot(hidden.pop(c), w2_ref[c * FFN_CHUNK:(c + 1) * FFN_CHUNK, :])
        if c == 0:
            acc_s[...] = alpha * x1_s[...] + part
        else:
            acc_s[...] += part

    def out_projection(j):
        r0, r1 = j * BLOCK, (j + 1) * BLOCK
        y_s[r0:r1, :] = alpha * x_ref[0, r0:r1, :] + _bdot(mix_s[r0:r1, :], w_out_ref[...])

    plan = iter(FILL_PLAN)

    def fill():
        item = next(plan)
        if item is None:
            return
        {"h": mlp_hidden, "o": mlp_output, "w": out_projection}[item[0]](int(item[1:]))

    seq_tile = jnp.clip(step, 0, last_tile) % n_tiles
    _mixer_outputs(seq_tile, n_tiles, sink_ref, lng_ref, lnb_ref, ws_ref, bs_ref, ng_ref,
                   q_s, k_s, v_s, gu_s, gv_s, bias_s, mix_s, fill)
    out_projection(tile // BLOCK - 1)


def _const_spec(shape):
    zeros = (0,) * len(shape)
    return pl.BlockSpec(shape, lambda *_: zeros, pipeline_mode=pl.Buffered(1))


def _block_call(x, sink, consts, alpha):
    batch, seq, d = x.shape
    n_tiles = seq // TILE
    n_steps = batch * n_tiles + PIPELINE_DEPTH - 1
    last_tile = batch * n_tiles - 1
    bpt = TILE // BLOCK
    n_seq_blocks = seq // BLOCK

    def tile_of(s, lag):
        t = jnp.clip(s - lag, 0, last_tile)
        return t // n_tiles, t % n_tiles

    def x_map(s):
        b, i = tile_of(s, 0)
        return b, i, 0

    def prev_map(s):
        b, i = tile_of(s, 0)
        return b, jnp.maximum(i * bpt - 1, 0), 0

    def next_map(s):
        b, i = tile_of(s, 0)
        return b, jnp.minimum((i + 1) * bpt, n_seq_blocks - 1), 0

    def out_map(s):
        b, i = tile_of(s, PIPELINE_DEPTH - 1)
        return b, i, 0

    kern = functools.partial(_block_kernel, alpha=alpha, n_tiles=n_tiles, n_steps=n_steps)
    in_specs = [
        pl.BlockSpec(memory_space=pltpu.SMEM),
        pl.BlockSpec((1, TILE, d), x_map),
        pl.BlockSpec((1, BLOCK, d), prev_map),
        pl.BlockSpec((1, BLOCK, d), next_map),
    ] + [_const_spec(c.shape) for c in consts]
    bf = jnp.bfloat16
    scratch = [
        pltpu.VMEM((TILE, ATTN_WIDTH), bf),
        pltpu.VMEM((4, TILE + 2 * BLOCK, LANES), bf),
        pltpu.VMEM((4, TILE + 2 * BLOCK, LANES), bf),
        pltpu.VMEM((TILE, GMLP_WIDTH), jnp.float32),
        pltpu.VMEM((TILE, GMLP_WIDTH), jnp.float32),
        pltpu.VMEM((N_Q_HEADS, BLOCK, 3 * BLOCK), jnp.float32),
        pltpu.VMEM((TILE, ATTN_WIDTH + GMLP_WIDTH), bf),
        pltpu.VMEM((TILE, d), jnp.float32),
        pltpu.VMEM((TILE, d), jnp.float32),
        pltpu.VMEM((TILE, d), bf),
        pltpu.VMEM((TILE, d), jnp.float32),
    ]
    return pl.pallas_call(
        kern,
        out_shape=jax.ShapeDtypeStruct(x.shape, x.dtype),
        grid=(n_steps,),
        in_specs=in_specs,
        out_specs=pl.BlockSpec((1, TILE, d), out_map),
        scratch_shapes=scratch,
        compiler_params=pltpu.CompilerParams(
            dimension_semantics=("arbitrary",),
            vmem_limit_bytes=VMEM_LIMIT_BYTES),
        name="block",
    )(sink, x, x, x, *consts)


def kernel(x, w_in, sink, gmlp_ln_g, gmlp_ln_b, w_spatial, b_spatial, attn_norm_g, gmlp_norm_g,
           w_out, ln1_g, ln1_b, w_ff1, w_ff2, ln2_g, ln2_b):
    batch, seq, d = x.shape
    depth = w_in.shape[0]
    assert seq % TILE == 0
    assert w_in.shape[2] == IN_WIDTH and w_spatial.shape[1:] == (GMLP_GROUPS, BLOCK, BLOCK)
    alpha = (2.0 * depth) ** 0.25
    bf = jnp.bfloat16
    row = lambda a: a.reshape(1, -1)
    for l in range(depth):
        ws_cat = w_spatial[l].reshape(GMLP_GROUPS // 2, 2, BLOCK, BLOCK)
        ws_cat = ws_cat.transpose(0, 2, 1, 3).reshape(GMLP_GROUPS // 2, BLOCK, 2 * BLOCK).astype(bf)
        bs_full = jnp.repeat(b_spatial[l].T, GMLP_WIDTH // GMLP_GROUPS, axis=1)
        norm_g = jnp.concatenate([attn_norm_g[l], gmlp_norm_g[l]]).reshape(1, -1)
        consts = (w_in[l].astype(bf), row(gmlp_ln_g[l]), row(gmlp_ln_b[l]), ws_cat, bs_full,
                  norm_g, w_out[l].astype(bf), row(ln1_g[l]), row(ln1_b[l]),
                  w_ff1[l].astype(bf), w_ff2[l].astype(bf), row(ln2_g[l]), row(ln2_b[l]))
        x = _block_call(x, sink[l], consts, alpha)
    return x
```

```python
import functools
import math

import jax
import jax.numpy as jnp
from jax import lax
from jax.experimental import pallas as pl
from jax.experimental.pallas import tpu as pltpu

HEAD_DIM = 64
N_Q_HEADS = 8
N_KV_HEADS = 2
Q_PER_KV = N_Q_HEADS // N_KV_HEADS
ATTN_WIDTH = N_Q_HEADS * HEAD_DIM
KV_WIDTH = N_KV_HEADS * HEAD_DIM
GMLP_WIDTH = 512
GMLP_GROUPS = 8
WINDOW = 128
BLOCK = 128
LN_EPS = 1e-5
NEG_INF = -1e30
LOG2E = math.log2(math.e)
LANES = 128

Q_OFF = 0
K_OFF = ATTN_WIDTH
V_OFF = ATTN_WIDTH + KV_WIDTH
GU_OFF = ATTN_WIDTH + 2 * KV_WIDTH
GV_OFF = GU_OFF + GMLP_WIDTH
IN_WIDTH = GV_OFF + GMLP_WIDTH

TILE = 512
FFN_CHUNK = 512
PIPELINE_DEPTH = 3
VMEM_LIMIT_BYTES = 60 * 1024 * 1024


def _bdot(a, b):
    return jnp.dot(a, b, preferred_element_type=jnp.float32)


def _gelu(x):
    return 0.5 * x * (1.0 + lax.erf(x * (0.5 ** 0.5)))


def _layer_norm(y, g, b):
    mu = jnp.mean(y, axis=-1, keepdims=True)
    yc = y - mu
    var = jnp.mean(yc * yc, axis=-1, keepdims=True)
    return yc * lax.rsqrt(var + LN_EPS) * g + b


def _rms_norm(y, g):
    ms = jnp.mean(y * y, axis=-1, keepdims=True)
    return y * lax.rsqrt(ms + LN_EPS) * g


def _half_masked(a, left):
    sw = pltpu.roll(a, HEAD_DIM, axis=1)
    zero = jnp.zeros_like(a)
    bf = jnp.bfloat16
    return (jnp.where(left, a, zero).astype(bf), jnp.where(left, zero, sw).astype(bf),
            jnp.where(left, sw, zero).astype(bf), jnp.where(left, zero, a).astype(bf))


def _fill_bias_table(bias_s):
    row = lax.broadcasted_iota(jnp.int32, (BLOCK, 3 * BLOCK), 0)
    col = lax.broadcasted_iota(jnp.int32, (BLOCK, 3 * BLOCK), 1)
    dist = jnp.abs(col - BLOCK - row)
    distf = dist.astype(jnp.float32)
    for h in range(N_Q_HEADS):
        slope = 2.0 ** (-8.0 * (h + 1) / N_Q_HEADS)
        bias_s[h] = jnp.where(dist <= WINDOW, (-slope * LOG2E) * distf, NEG_INF)


def _store_kv(k_s, v_s, start, kv_rows):
    n = kv_rows.shape[0]
    left = lax.broadcasted_iota(jnp.int32, (n, LANES), 1) < HEAD_DIM
    for i, a in enumerate(_half_masked(kv_rows[:, :KV_WIDTH], left)):
        k_s[i, start:start + n, :] = a
    for i, a in enumerate(_half_masked(kv_rows[:, KV_WIDTH:], left)):
        v_s[i, start:start + n, :] = a


def _mixer_outputs(seq_tile, n_tiles, sink_ref, lng_ref, lnb_ref, ws_ref, bs_ref, ng_ref,
                   q_s, k_s, v_s, gu_s, gv_s, bias_s, mix_s, fill):
    n_blocks = q_s.shape[0] // BLOCK
    bf = jnp.bfloat16
    col = lax.broadcasted_iota(jnp.int32, (BLOCK, BLOCK), 1)
    lane_left = col < HEAD_DIM
    first_valid = jnp.where(seq_tile == 0, BLOCK, 0)
    last_valid = jnp.where(seq_tile == n_tiles - 1, 0, BLOCK)

    for j in range(n_blocks):
        r0, r1 = j * BLOCK, (j + 1) * BLOCK
        k0, k1 = j * BLOCK, (j + 3) * BLOCK
        attn_parts = []
        for g in range(N_KV_HEADS):
            lhs = jnp.concatenate([q_s[r0:r1, g * 256:g * 256 + LANES],
                                   q_s[r0:r1, g * 256 + LANES:(g + 1) * 256]], axis=0)
            kk = jnp.concatenate([k_s[2 * g, k0:k1, :], k_s[2 * g + 1, k0:k1, :]], axis=0)
            s = lax.dot_general(lhs, kk, (((1,), (1,)), ((), ())),
                                preferred_element_type=jnp.float32)
            fill()
            p_par = []
            for par in range(2):
                p_half = []
                for half in range(2):
                    h = Q_PER_KV * g + 2 * half + par
                    sink = sink_ref[h] * LOG2E
                    sb = s[half * BLOCK:(half + 1) * BLOCK,
                           par * 3 * BLOCK:(par + 1) * 3 * BLOCK] + bias_s[h]
                    if j == 0:
                        sb = jnp.concatenate(
                            [jnp.where(col >= first_valid, sb[:, :BLOCK], NEG_INF),
                             sb[:, BLOCK:]], axis=1)
                    if j == n_blocks - 1:
                        sb = jnp.concatenate(
                            [sb[:, :2 * BLOCK],
                             jnp.where(col < last_valid, sb[:, 2 * BLOCK:], NEG_INF)], axis=1)
                    m = jnp.maximum(jnp.max(sb, axis=-1, keepdims=True), sink)
                    e = jnp.exp2(sb - m)
                    den = jnp.sum(e, axis=-1, keepdims=True) + jnp.exp2(sink - m)
                    p_half.append((e * (1.0 / den)).astype(bf))
                p_par.append(jnp.concatenate(p_half, axis=0))
            p = jnp.concatenate(p_par, axis=1)
            vv = jnp.concatenate([v_s[2 * g, k0:k1, :], v_s[2 * g + 1, k0:k1, :]], axis=0)
            o = _bdot(p, vv)
            fill()
            attn_parts += [o[:BLOCK], o[BLOCK:]]
        attn = jnp.concatenate(attn_parts, axis=1)
        mix_s[r0:r1, 0:ATTN_WIDTH] = _rms_norm(attn, ng_ref[:, 0:ATTN_WIDTH]).astype(bf)

        vn = _layer_norm(_gelu(gv_s[r0:r1, :]), lng_ref[...], lnb_ref[...])
        mixed_parts = []
        for pair in range(GMLP_GROUPS // 2):
            vp = vn[:, pair * LANES:(pair + 1) * LANES]
            zero = jnp.zeros_like(vp)
            rhs = jnp.concatenate([jnp.where(lane_left, vp, zero),
                                   jnp.where(lane_left, zero, vp)], axis=0).astype(bf)
            mixed_parts.append(_bdot(ws_ref[pair], rhs))
        fill()
        mixed = jnp.concatenate(mixed_parts, axis=1) + bs_ref[...]
        sgu = _gelu(gu_s[r0:r1, :]) * mixed
        mix_s[r0:r1, ATTN_WIDTH:] = _rms_norm(sgu, ng_ref[:, ATTN_WIDTH:]).astype(bf)


HOOKS_PER_BLOCK = 5

FILL_PLAN = (
    "h0 h1", None, "o0 h2", None, "o1",
    "w0 h3", None, "o2 h4", None, "o3",
    "w1 h5", None, "o4 h6", None, "o5",
    "w2 h7", None, "o6", None, "o7",
    "w3",
)


def _block_kernel(sink_ref, x_ref, xp_ref, xn_ref, w_in_ref, lng_ref, lnb_ref, ws_ref, bs_ref,
                  ng_ref, w_out_ref, ln1g_ref, ln1b_ref, w1_ref, w2_ref, ln2g_ref, ln2b_ref,
                  o_ref, q_s, k_s, v_s, gu_s, gv_s, bias_s, mix_s, y_s, x1_s, x1b_s, acc_s,
                  *, alpha, n_tiles, n_steps):
    step = pl.program_id(0)
    last_tile = n_steps - PIPELINE_DEPTH
    tile = x_ref.shape[1]
    bf = jnp.bfloat16
    assert len(FILL_PLAN) == HOOKS_PER_BLOCK * (tile // BLOCK) + 1
    assert w1_ref.shape[1] == 8 * FFN_CHUNK and tile == 4 * BLOCK

    @pl.when(step == 0)
    def _():
        _fill_bias_table(bias_s)
        y_s[...] = jnp.zeros(y_s.shape, y_s.dtype)
        acc_s[...] = jnp.zeros(acc_s.shape, acc_s.dtype)

    xb = x_ref[0].astype(bf)
    xh = jnp.concatenate([xp_ref[0], xn_ref[0]], axis=0).astype(bf)
    kv = _bdot(xb, w_in_ref[:, K_OFF:GU_OFF])
    kvh = _bdot(xh, w_in_ref[:, K_OFF:GU_OFF])
    q_s[...] = (_bdot(xb, w_in_ref[:, Q_OFF:K_OFF]) * (LOG2E * HEAD_DIM ** -0.5)).astype(bf)
    gu_s[...] = _bdot(xb, w_in_ref[:, GU_OFF:GV_OFF])
    gv_s[...] = _bdot(xb, w_in_ref[:, GV_OFF:IN_WIDTH])
    _store_kv(k_s, v_s, 0, kvh[:BLOCK])
    _store_kv(k_s, v_s, BLOCK, kv)
    _store_kv(k_s, v_s, BLOCK + tile, kvh[BLOCK:])

    o_ref[0] = _layer_norm(acc_s[...], ln2g_ref[...], ln2b_ref[...])
    x1 = _layer_norm(y_s[...], ln1g_ref[...], ln1b_ref[...])
    x1_s[...] = x1
    x1b_s[...] = x1.astype(bf)

    hidden = {}

    def mlp_hidden(c):
        h = jnp.maximum(_bdot(x1b_s[...], w1_ref[:, c * FFN_CHUNK:(c + 1) * FFN_CHUNK]), 0.0)
        hidden[c] = (h * h).astype(bf)

    def mlp_output(c):
        part = _bdot(hidden.pop(c), w2_ref[c * FFN_CHUNK:(c + 1) * FFN_CHUNK, :])
        if c == 0:
            acc_s[...] = alpha * x1_s[...] + part
        else:
            acc_s[...] += part

    def out_projection(j):
        r0, r1 = j * BLOCK, (j + 1) * BLOCK
        y_s[r0:r1, :] = alpha * x_ref[0, r0:r1, :] + _bdot(mix_s[r0:r1, :], w_out_ref[...])

    plan = iter(FILL_PLAN)

    def fill():
        for item in (next(plan) or "").split():
            {"h": mlp_hidden, "o": mlp_output, "w": out_projection}[item[0]](int(item[1:]))

    seq_tile = jnp.clip(step, 0, last_tile) % n_tiles
    _mixer_outputs(seq_tile, n_tiles, sink_ref, lng_ref, lnb_ref, ws_ref, bs_ref, ng_ref,
                   q_s, k_s, v_s, gu_s, gv_s, bias_s, mix_s, fill)
    fill()


def _const_spec(shape):
    zeros = (0,) * len(shape)
    return pl.BlockSpec(shape, lambda *_: zeros, pipeline_mode=pl.Buffered(1))


def _block_call(x, sink, consts, alpha):
    batch, seq, d = x.shape
    n_tiles = seq // TILE
    n_steps = batch * n_tiles + PIPELINE_DEPTH - 1
    last_tile = batch * n_tiles - 1
    bpt = TILE // BLOCK
    n_seq_blocks = seq // BLOCK

    def tile_of(s, lag):
        t = jnp.clip(s - lag, 0, last_tile)
        return t // n_tiles, t % n_tiles

    def x_map(s):
        b, i = tile_of(s, 0)
        return b, i, 0

    def prev_map(s):
        b, i = tile_of(s, 0)
        return b, jnp.maximum(i * bpt - 1, 0), 0

    def next_map(s):
        b, i = tile_of(s, 0)
        return b, jnp.minimum((i + 1) * bpt, n_seq_blocks - 1), 0

    def out_map(s):
        b, i = tile_of(s, PIPELINE_DEPTH - 1)
        return b, i, 0

    kern = functools.partial(_block_kernel, alpha=alpha, n_tiles=n_tiles, n_steps=n_steps)
    in_specs = [
        pl.BlockSpec(memory_space=pltpu.SMEM),
        pl.BlockSpec((1, TILE, d), x_map),
        pl.BlockSpec((1, BLOCK, d), prev_map),
        pl.BlockSpec((1, BLOCK, d), next_map),
    ] + [_const_spec(c.shape) for c in consts]
    bf = jnp.bfloat16
    scratch = [
        pltpu.VMEM((TILE, ATTN_WIDTH), bf),
        pltpu.VMEM((4, TILE + 2 * BLOCK, LANES), bf),
        pltpu.VMEM((4, TILE + 2 * BLOCK, LANES), bf),
        pltpu.VMEM((TILE, GMLP_WIDTH), jnp.float32),
        pltpu.VMEM((TILE, GMLP_WIDTH), jnp.float32),
        pltpu.VMEM((N_Q_HEADS, BLOCK, 3 * BLOCK), jnp.float32),
        pltpu.VMEM((TILE, ATTN_WIDTH + GMLP_WIDTH), bf),
        pltpu.VMEM((TILE, d), jnp.float32),
        pltpu.VMEM((TILE, d), jnp.float32),
        pltpu.VMEM((TILE, d), bf),
        pltpu.VMEM((TILE, d), jnp.float32),
    ]
    return pl.pallas_call(
        kern,
        out_shape=jax.ShapeDtypeStruct(x.shape, x.dtype),
        grid=(n_steps,),
        in_specs=in_specs,
        out_specs=pl.BlockSpec((1, TILE, d), out_map),
        scratch_shapes=scratch,
        compiler_params=pltpu.CompilerParams(
            dimension_semantics=("arbitrary",),
            vmem_limit_bytes=VMEM_LIMIT_BYTES),
        name="block",
    )(sink, x, x, x, *consts)


def kernel(x, w_in, sink, gmlp_ln_g, gmlp_ln_b, w_spatial, b_spatial, attn_norm_g, gmlp_norm_g,
           w_out, ln1_g, ln1_b, w_ff1, w_ff2, ln2_g, ln2_b):
    batch, seq, d = x.shape
    depth = w_in.shape[0]
    assert seq % TILE == 0
    assert w_in.shape[2] == IN_WIDTH and w_spatial.shape[1:] == (GMLP_GROUPS, BLOCK, BLOCK)
    alpha = (2.0 * depth) ** 0.25
    bf = jnp.bfloat16
    row = lambda a: a.reshape(1, -1)
    for l in range(depth):
        ws_cat = w_spatial[l].reshape(GMLP_GROUPS // 2, 2, BLOCK, BLOCK)
        ws_cat = ws_cat.transpose(0, 2, 1, 3).reshape(GMLP_GROUPS // 2, BLOCK, 2 * BLOCK).astype(bf)
        bs_full = jnp.repeat(b_spatial[l].T, GMLP_WIDTH // GMLP_GROUPS, axis=1)
        norm_g = jnp.concatenate([attn_norm_g[l], gmlp_norm_g[l]]).reshape(1, -1)
        consts = (w_in[l].astype(bf), row(gmlp_ln_g[l]), row(gmlp_ln_b[l]), ws_cat, bs_full,
                  norm_g, w_out[l].astype(bf), row(ln1_g[l]), row(ln1_b[l]),
                  w_ff1[l].astype(bf), w_ff2[l].astype(bf), row(ln2_g[l]), row(ln2_b[l]))
        x = _block_call(x, sink[l], consts, alpha)
    return x
```

```python
import functools
import math

import jax
import jax.numpy as jnp
from jax import lax
from jax.experimental import pallas as pl
from jax.experimental.pallas import tpu as pltpu

HEAD_DIM = 64
N_Q_HEADS = 8
N_KV_HEADS = 2
Q_PER_KV = N_Q_HEADS // N_KV_HEADS
ATTN_WIDTH = N_Q_HEADS * HEAD_DIM
KV_WIDTH = N_KV_HEADS * HEAD_DIM
GMLP_WIDTH = 512
GMLP_GROUPS = 8
WINDOW = 128
BLOCK = 128
LN_EPS = 1e-5
NEG_INF = -1e30
LOG2E = math.log2(math.e)
LANES = 128

Q_OFF = 0
K_OFF = ATTN_WIDTH
V_OFF = ATTN_WIDTH + KV_WIDTH
GU_OFF = ATTN_WIDTH + 2 * KV_WIDTH
GV_OFF = GU_OFF + GMLP_WIDTH
IN_WIDTH = GV_OFF + GMLP_WIDTH

TILE = 512
FFN_CHUNK = 512
PIPELINE_DEPTH = 3
VMEM_LIMIT_BYTES = 60 * 1024 * 1024


def _bdot(a, b):
    return jnp.dot(a, b, preferred_element_type=jnp.float32)


def _gelu(x):
    return 0.5 * x * (1.0 + lax.erf(x * (0.5 ** 0.5)))


def _layer_norm(y, g, b):
    mu = jnp.mean(y, axis=-1, keepdims=True)
    yc = y - mu
    var = jnp.mean(yc * yc, axis=-1, keepdims=True)
    return yc * lax.rsqrt(var + LN_EPS) * g + b


def _rms_norm(y, g):
    ms = jnp.mean(y * y, axis=-1, keepdims=True)
    return y * lax.rsqrt(ms + LN_EPS) * g


def _half_masked(a, left):
    sw = pltpu.roll(a, HEAD_DIM, axis=1)
    zero = jnp.zeros_like(a)
    bf = jnp.bfloat16
    return (jnp.where(left, a, zero).astype(bf), jnp.where(left, zero, sw).astype(bf),
            jnp.where(left, sw, zero).astype(bf), jnp.where(left, zero, a).astype(bf))


def _fill_bias_table(bias_s):
    row = lax.broadcasted_iota(jnp.int32, (BLOCK, 3 * BLOCK), 0)
    col = lax.broadcasted_iota(jnp.int32, (BLOCK, 3 * BLOCK), 1)
    dist = jnp.abs(col - BLOCK - row)
    distf = dist.astype(jnp.float32)
    for h in range(N_Q_HEADS):
        slope = 2.0 ** (-8.0 * (h + 1) / N_Q_HEADS)
        bias_s[h] = jnp.where(dist <= WINDOW, (-slope * LOG2E) * distf, NEG_INF)


def _store_kv(k_s, v_s, start, kv_rows):
    n = kv_rows.shape[0]
    bf = jnp.bfloat16
    kt = kv_rows[:, :KV_WIDTH].T
    kt_sw = pltpu.roll(kt, HEAD_DIM, axis=0)
    top = lax.broadcasted_iota(jnp.int32, (KV_WIDTH, n), 0) < HEAD_DIM
    zero = jnp.zeros_like(kt)
    for i, a in enumerate((jnp.where(top, kt, zero), jnp.where(top, zero, kt_sw),
                           jnp.where(top, kt_sw, zero), jnp.where(top, zero, kt))):
        k_s[i, :, start:start + n] = a.astype(bf)
    left = lax.broadcasted_iota(jnp.int32, (n, LANES), 1) < HEAD_DIM
    for i, a in enumerate(_half_masked(kv_rows[:, KV_WIDTH:], left)):
        v_s[i, start:start + n, :] = a


def _mixer_outputs(seq_tile, n_tiles, sink_ref, lng_ref, lnb_ref, ws_ref, bs_ref, ng_ref,
                   q_s, k_s, v_s, gu_s, gv_s, bias_s, mix_s, fill):
    n_blocks = q_s.shape[0] // BLOCK
    bf = jnp.bfloat16
    col = lax.broadcasted_iota(jnp.int32, (BLOCK, BLOCK), 1)
    lane_left = col < HEAD_DIM
    first_valid = jnp.where(seq_tile == 0, BLOCK, 0)
    last_valid = jnp.where(seq_tile == n_tiles - 1, 0, BLOCK)

    for j in range(n_blocks):
        r0, r1 = j * BLOCK, (j + 1) * BLOCK
        k0, k1 = j * BLOCK, (j + 3) * BLOCK
        attn_parts = []
        for g in range(N_KV_HEADS):
            lhs = jnp.concatenate([q_s[r0:r1, g * 256:g * 256 + LANES],
                                   q_s[r0:r1, g * 256 + LANES:(g + 1) * 256]], axis=0)
            kk = jnp.concatenate([k_s[2 * g, :, k0:k1], k_s[2 * g + 1, :, k0:k1]], axis=1)
            s = _bdot(lhs, kk)
            fill()
            p_par = []
            for par in range(2):
                p_half = []
                for half in range(2):
                    h = Q_PER_KV * g + 2 * half + par
                    sink = sink_ref[h] * LOG2E
                    sb = s[half * BLOCK:(half + 1) * BLOCK,
                           par * 3 * BLOCK:(par + 1) * 3 * BLOCK] + bias_s[h]
                    if j == 0:
                        sb = jnp.concatenate(
                            [jnp.where(col >= first_valid, sb[:, :BLOCK], NEG_INF),
                             sb[:, BLOCK:]], axis=1)
                    if j == n_blocks - 1:
                        sb = jnp.concatenate(
                            [sb[:, :2 * BLOCK],
                             jnp.where(col < last_valid, sb[:, 2 * BLOCK:], NEG_INF)], axis=1)
                    m = jnp.maximum(jnp.max(sb, axis=-1, keepdims=True), sink)
                    e = jnp.exp2(sb - m)
                    den = jnp.sum(e, axis=-1, keepdims=True) + jnp.exp2(sink - m)
                    p_half.append((e * (1.0 / den)).astype(bf))
                p_par.append(jnp.concatenate(p_half, axis=0))
            p = jnp.concatenate(p_par, axis=1)
            vv = jnp.concatenate([v_s[2 * g, k0:k1, :], v_s[2 * g + 1, k0:k1, :]], axis=0)
            o = _bdot(p, vv)
            fill()
            attn_parts += [o[:BLOCK], o[BLOCK:]]
        attn = jnp.concatenate(attn_parts, axis=1)
        mix_s[r0:r1, 0:ATTN_WIDTH] = _rms_norm(attn, ng_ref[:, 0:ATTN_WIDTH]).astype(bf)

        vn = _layer_norm(_gelu(gv_s[r0:r1, :]), lng_ref[...], lnb_ref[...])
        mixed_parts = []
        for pair in range(GMLP_GROUPS // 2):
            vp = vn[:, pair * LANES:(pair + 1) * LANES]
            zero = jnp.zeros_like(vp)
            rhs = jnp.concatenate([jnp.where(lane_left, vp, zero),
                                   jnp.where(lane_left, zero, vp)], axis=0).astype(bf)
            mixed_parts.append(_bdot(ws_ref[pair], rhs))
        fill()
        mixed = jnp.concatenate(mixed_parts, axis=1) + bs_ref[...]
        sgu = _gelu(gu_s[r0:r1, :]) * mixed
        mix_s[r0:r1, ATTN_WIDTH:] = _rms_norm(sgu, ng_ref[:, ATTN_WIDTH:]).astype(bf)


HOOKS_PER_BLOCK = 5

FILL_PLAN = (
    "h0 h1", None, "o0 h2", None, "o1",
    "w0 h3", None, "o2 h4", None, "o3",
    "w1 h5", None, "o4 h6", None, "o5",
    "w2 h7", None, "o6", None, "o7",
    "w3",
)


def _block_kernel(sink_ref, x_ref, xp_ref, xn_ref, w_in_ref, lng_ref, lnb_ref, ws_ref, bs_ref,
                  ng_ref, w_out_ref, ln1g_ref, ln1b_ref, w1_ref, w2_ref, ln2g_ref, ln2b_ref,
                  o_ref, q_s, k_s, v_s, gu_s, gv_s, bias_s, mix_s, y_s, x1_s, x1b_s, acc_s,
                  *, alpha, n_tiles, n_steps):
    step = pl.program_id(0)
    last_tile = n_steps - PIPELINE_DEPTH
    tile = x_ref.shape[1]
    bf = jnp.bfloat16
    assert len(FILL_PLAN) == HOOKS_PER_BLOCK * (tile // BLOCK) + 1
    assert w1_ref.shape[1] == 8 * FFN_CHUNK and tile == 4 * BLOCK

    @pl.when(step == 0)
    def _():
        _fill_bias_table(bias_s)
        y_s[...] = jnp.zeros(y_s.shape, y_s.dtype)
        acc_s[...] = jnp.zeros(acc_s.shape, acc_s.dtype)

    xb = x_ref[0].astype(bf)
    xh = jnp.concatenate([xp_ref[0], xn_ref[0]], axis=0).astype(bf)
    kv = _bdot(xb, w_in_ref[:, K_OFF:GU_OFF])
    kvh = _bdot(xh, w_in_ref[:, K_OFF:GU_OFF])
    q_s[...] = (_bdot(xb, w_in_ref[:, Q_OFF:K_OFF]) * (LOG2E * HEAD_DIM ** -0.5)).astype(bf)
    gu_s[...] = _bdot(xb, w_in_ref[:, GU_OFF:GV_OFF])
    gv_s[...] = _bdot(xb, w_in_ref[:, GV_OFF:IN_WIDTH])
    _store_kv(k_s, v_s, 0, kvh[:BLOCK])
    _store_kv(k_s, v_s, BLOCK, kv)
    _store_kv(k_s, v_s, BLOCK + tile, kvh[BLOCK:])

    o_ref[0] = _layer_norm(acc_s[...], ln2g_ref[...], ln2b_ref[...])
    x1 = _layer_norm(y_s[...], ln1g_ref[...], ln1b_ref[...])
    x1_s[...] = x1
    x1b_s[...] = x1.astype(bf)

    hidden = {}

    def mlp_hidden(c):
        h = jnp.maximum(_bdot(x1b_s[...], w1_ref[:, c * FFN_CHUNK:(c + 1) * FFN_CHUNK]), 0.0)
        hidden[c] = (h * h).astype(bf)

    def mlp_output(c):
        part = _bdot(hidden.pop(c), w2_ref[c * FFN_CHUNK:(c + 1) * FFN_CHUNK, :])
        if c == 0:
            acc_s[...] = alpha * x1_s[...] + part
        else:
            acc_s[...] += part

    def out_projection(j):
        r0, r1 = j * BLOCK, (j + 1) * BLOCK
        y_s[r0:r1, :] = alpha * x_ref[0, r0:r1, :] + _bdot(mix_s[r0:r1, :], w_out_ref[...])

    plan = iter(FILL_PLAN)

    def fill():
        for item in (next(plan) or "").split():
            {"h": mlp_hidden, "o": mlp_output, "w": out_projection}[item[0]](int(item[1:]))

    seq_tile = jnp.clip(step, 0, last_tile) % n_tiles
    _mixer_outputs(seq_tile, n_tiles, sink_ref, lng_ref, lnb_ref, ws_ref, bs_ref, ng_ref,
                   q_s, k_s, v_s, gu_s, gv_s, bias_s, mix_s, fill)
    fill()


def _const_spec(shape):
    zeros = (0,) * len(shape)
    return pl.BlockSpec(shape, lambda *_: zeros, pipeline_mode=pl.Buffered(1))


def _block_call(x, sink, consts, alpha):
    batch, seq, d = x.shape
    n_tiles = seq // TILE
    n_steps = batch * n_tiles + PIPELINE_DEPTH - 1
    last_tile = batch * n_tiles - 1
    bpt = TILE // BLOCK
    n_seq_blocks = seq // BLOCK

    def tile_of(s, lag):
        t = jnp.clip(s - lag, 0, last_tile)
        return t // n_tiles, t % n_tiles

    def x_map(s):
        b, i = tile_of(s, 0)
        return b, i, 0

    def prev_map(s):
        b, i = tile_of(s, 0)
        return b, jnp.maximum(i * bpt - 1, 0), 0

    def next_map(s):
        b, i = tile_of(s, 0)
        return b, jnp.minimum((i + 1) * bpt, n_seq_blocks - 1), 0

    def out_map(s):
        b, i = tile_of(s, PIPELINE_DEPTH - 1)
        return b, i, 0

    kern = functools.partial(_block_kernel, alpha=alpha, n_tiles=n_tiles, n_steps=n_steps)
    in_specs = [
        pl.BlockSpec(memory_space=pltpu.SMEM),
        pl.BlockSpec((1, TILE, d), x_map),
        pl.BlockSpec((1, BLOCK, d), prev_map),
        pl.BlockSpec((1, BLOCK, d), next_map),
    ] + [_const_spec(c.shape) for c in consts]
    bf = jnp.bfloat16
    scratch = [
        pltpu.VMEM((TILE, ATTN_WIDTH), bf),
        pltpu.VMEM((4, LANES, TILE + 2 * BLOCK), bf),
        pltpu.VMEM((4, TILE + 2 * BLOCK, LANES), bf),
        pltpu.VMEM((TILE, GMLP_WIDTH), jnp.float32),
        pltpu.VMEM((TILE, GMLP_WIDTH), jnp.float32),
        pltpu.VMEM((N_Q_HEADS, BLOCK, 3 * BLOCK), jnp.float32),
        pltpu.VMEM((TILE, ATTN_WIDTH + GMLP_WIDTH), bf),
        pltpu.VMEM((TILE, d), jnp.float32),
        pltpu.VMEM((TILE, d), jnp.float32),
        pltpu.VMEM((TILE, d), bf),
        pltpu.VMEM((TILE, d), jnp.float32),
    ]
    return pl.pallas_call(
        kern,
        out_shape=jax.ShapeDtypeStruct(x.shape, x.dtype),
        grid=(n_steps,),
        in_specs=in_specs,
        out_specs=pl.BlockSpec((1, TILE, d), out_map),
        scratch_shapes=scratch,
        compiler_params=pltpu.CompilerParams(
            dimension_semantics=("arbitrary",),
            vmem_limit_bytes=VMEM_LIMIT_BYTES),
        name="block",
    )(sink, x, x, x, *consts)


def kernel(x, w_in, sink, gmlp_ln_g, gmlp_ln_b, w_spatial, b_spatial, attn_norm_g, gmlp_norm_g,
           w_out, ln1_g, ln1_b, w_ff1, w_ff2, ln2_g, ln2_b):
    batch, seq, d = x.shape
    depth = w_in.shape[0]
    assert seq % TILE == 0
    assert w_in.shape[2] == IN_WIDTH and w_spatial.shape[1:] == (GMLP_GROUPS, BLOCK, BLOCK)
    alpha = (2.0 * depth) ** 0.25
    bf = jnp.bfloat16
    row = lambda a: a.reshape(1, -1)
    for l in range(depth):
        ws_cat = w_spatial[l].reshape(GMLP_GROUPS // 2, 2, BLOCK, BLOCK)
        ws_cat = ws_cat.transpose(0, 2, 1, 3).reshape(GMLP_GROUPS // 2, BLOCK, 2 * BLOCK).astype(bf)
        bs_full = jnp.repeat(b_spatial[l].T, GMLP_WIDTH // GMLP_GROUPS, axis=1)
        norm_g = jnp.concatenate([attn_norm_g[l], gmlp_norm_g[l]]).reshape(1, -1)
        consts = (w_in[l].astype(bf), row(gmlp_ln_g[l]), row(gmlp_ln_b[l]), ws_cat, bs_full,
                  norm_g, w_out[l].astype(bf), row(ln1_g[l]), row(ln1_b[l]),
                  w_ff1[l].astype(bf), w_ff2[l].astype(bf), row(ln2_g[l]), row(ln2_b[l]))
        x = _block_call(x, sink[l], consts, alpha)
    return x
```

```python
import functools
import math

import jax
import jax.numpy as jnp
from jax import lax
from jax.experimental import pallas as pl
from jax.experimental.pallas import tpu as pltpu

HEAD_DIM = 64
N_Q_HEADS = 8
N_KV_HEADS = 2
Q_PER_KV = N_Q_HEADS // N_KV_HEADS
ATTN_WIDTH = N_Q_HEADS * HEAD_DIM
KV_WIDTH = N_KV_HEADS * HEAD_DIM
GMLP_WIDTH = 512
GMLP_GROUPS = 8
WINDOW = 128
BLOCK = 128
LN_EPS = 1e-5
NEG_INF = -1e30
LOG2E = math.log2(math.e)
LANES = 128

Q_OFF = 0
K_OFF = ATTN_WIDTH
V_OFF = ATTN_WIDTH + KV_WIDTH
GU_OFF = ATTN_WIDTH + 2 * KV_WIDTH
GV_OFF = GU_OFF + GMLP_WIDTH
IN_WIDTH = GV_OFF + GMLP_WIDTH

TILE = 512
FFN_CHUNK = 512
PIPELINE_DEPTH = 3
VMEM_LIMIT_BYTES = 60 * 1024 * 1024


def _bdot(a, b):
    return jnp.dot(a, b, preferred_element_type=jnp.float32)


def _gelu(x):
    return 0.5 * x * (1.0 + lax.erf(x * (0.5 ** 0.5)))


def _layer_norm(y, g, b):
    mu = jnp.mean(y, axis=-1, keepdims=True)
    yc = y - mu
    var = jnp.mean(yc * yc, axis=-1, keepdims=True)
    return yc * lax.rsqrt(var + LN_EPS) * g + b


def _rms_norm(y, g):
    ms = jnp.mean(y * y, axis=-1, keepdims=True)
    return y * lax.rsqrt(ms + LN_EPS) * g


def _half_masked(a, left):
    sw = pltpu.roll(a, HEAD_DIM, axis=1)
    zero = jnp.zeros_like(a)
    bf = jnp.bfloat16
    return (jnp.where(left, a, zero).astype(bf), jnp.where(left, zero, sw).astype(bf),
            jnp.where(left, sw, zero).astype(bf), jnp.where(left, zero, a).astype(bf))


def _fill_bias_table(bias_s):
    key = lax.broadcasted_iota(jnp.int32, (3 * BLOCK, BLOCK), 0)
    query = lax.broadcasted_iota(jnp.int32, (3 * BLOCK, BLOCK), 1)
    dist = jnp.abs(key - BLOCK - query)
    distf = dist.astype(jnp.float32)
    for h in range(N_Q_HEADS):
        slope = 2.0 ** (-8.0 * (h + 1) / N_Q_HEADS)
        bias_s[h] = jnp.where(dist <= WINDOW, (-slope * LOG2E) * distf, NEG_INF)


def _store_kv(k_s, v_s, start, kv_rows):
    n = kv_rows.shape[0]
    bf = jnp.bfloat16
    left = lax.broadcasted_iota(jnp.int32, (n, LANES), 1) < HEAD_DIM
    for i, a in enumerate(_half_masked(kv_rows[:, :KV_WIDTH], left)):
        k_s[i, start:start + n, :] = a
    vt = kv_rows[:, KV_WIDTH:].T
    vt_sw = pltpu.roll(vt, HEAD_DIM, axis=0)
    top = lax.broadcasted_iota(jnp.int32, (KV_WIDTH, n), 0) < HEAD_DIM
    zero = jnp.zeros_like(vt)
    for i, a in enumerate((jnp.where(top, vt, zero), jnp.where(top, zero, vt_sw),
                           jnp.where(top, vt_sw, zero), jnp.where(top, zero, vt))):
        v_s[i, :, start:start + n] = a.astype(bf)


def _mixer_outputs(seq_tile, n_tiles, sink_ref, lng_ref, lnb_ref, ws_ref, bs_ref, ng_ref,
                   q_s, k_s, v_s, gu_s, gv_s, bias_s, mix_s, fill):
    n_blocks = q_s.shape[0] // BLOCK
    bf = jnp.bfloat16
    lane_left = lax.broadcasted_iota(jnp.int32, (BLOCK, BLOCK), 1) < HEAD_DIM
    has_prev = seq_tile != 0
    has_next = seq_tile != n_tiles - 1

    for j in range(n_blocks):
        r0, r1 = j * BLOCK, (j + 1) * BLOCK
        k0, k1 = j * BLOCK, (j + 3) * BLOCK
        attn_parts = []
        for g in range(N_KV_HEADS):
            lhs = jnp.concatenate([q_s[r0:r1, g * 256:g * 256 + LANES],
                                   q_s[r0:r1, g * 256 + LANES:(g + 1) * 256]], axis=0)
            kk = jnp.concatenate([k_s[2 * g, k0:k1, :], k_s[2 * g + 1, k0:k1, :]], axis=0)
            st = lax.dot_general(kk, lhs, (((1,), (1,)), ((), ())),
                                 preferred_element_type=jnp.float32)
            fill()
            p_par = []
            for par in range(2):
                p_half = []
                for half in range(2):
                    h = Q_PER_KV * g + 2 * half + par
                    sink = sink_ref[h] * LOG2E
                    sb = st[par * 3 * BLOCK:(par + 1) * 3 * BLOCK,
                            half * BLOCK:(half + 1) * BLOCK] + bias_s[h]
                    if j == 0:
                        sb = jnp.concatenate(
                            [jnp.where(has_prev, sb[:BLOCK], NEG_INF), sb[BLOCK:]], axis=0)
                    if j == n_blocks - 1:
                        sb = jnp.concatenate(
                            [sb[:2 * BLOCK], jnp.where(has_next, sb[2 * BLOCK:], NEG_INF)], axis=0)
                    m = jnp.maximum(jnp.max(sb, axis=0, keepdims=True), sink)
                    e = jnp.exp2(sb - m)
                    den = jnp.sum(e, axis=0, keepdims=True) + jnp.exp2(sink - m)
                    p_half.append((e * (1.0 / den)).astype(bf))
                p_par.append(jnp.concatenate(p_half, axis=1))
            pt = jnp.concatenate(p_par, axis=0)
            vvt = jnp.concatenate([v_s[2 * g, :, k0:k1], v_s[2 * g + 1, :, k0:k1]], axis=1)
            ot = _bdot(vvt, pt)
            fill()
            attn_parts += [ot[:, :BLOCK].T, ot[:, BLOCK:].T]
        attn = jnp.concatenate(attn_parts, axis=1)
        mix_s[r0:r1, 0:ATTN_WIDTH] = _rms_norm(attn, ng_ref[:, 0:ATTN_WIDTH]).astype(bf)

        vn = _layer_norm(_gelu(gv_s[r0:r1, :]), lng_ref[...], lnb_ref[...])
        mixed_parts = []
        for pair in range(GMLP_GROUPS // 2):
            vp = vn[:, pair * LANES:(pair + 1) * LANES]
            zero = jnp.zeros_like(vp)
            rhs = jnp.concatenate([jnp.where(lane_left, vp, zero),
                                   jnp.where(lane_left, zero, vp)], axis=0).astype(bf)
            mixed_parts.append(_bdot(ws_ref[pair], rhs))
        fill()
        mixed = jnp.concatenate(mixed_parts, axis=1) + bs_ref[...]
        sgu = _gelu(gu_s[r0:r1, :]) * mixed
        mix_s[r0:r1, ATTN_WIDTH:] = _rms_norm(sgu, ng_ref[:, ATTN_WIDTH:]).astype(bf)


HOOKS_PER_BLOCK = 5

FILL_PLAN = (
    "h0 h1", None, "o0 h2", None, "o1",
    "w0 h3", None, "o2 h4", None, "o3",
    "w1 h5", None, "o4 h6", None, "o5",
    "w2 h7", None, "o6", None, "o7",
    "w3",
)


def _block_kernel(sink_ref, x_ref, xp_ref, xn_ref, w_in_ref, lng_ref, lnb_ref, ws_ref, bs_ref,
                  ng_ref, w_out_ref, ln1g_ref, ln1b_ref, w1_ref, w2_ref, ln2g_ref, ln2b_ref,
                  o_ref, q_s, k_s, v_s, gu_s, gv_s, bias_s, mix_s, y_s, x1_s, x1b_s, acc_s,
                  *, alpha, n_tiles, n_steps):
    step = pl.program_id(0)
    last_tile = n_steps - PIPELINE_DEPTH
    tile = x_ref.shape[1]
    bf = jnp.bfloat16
    assert len(FILL_PLAN) == HOOKS_PER_BLOCK * (tile // BLOCK) + 1
    assert w1_ref.shape[1] == 8 * FFN_CHUNK and tile == 4 * BLOCK

    @pl.when(step == 0)
    def _():
        _fill_bias_table(bias_s)
        y_s[...] = jnp.zeros(y_s.shape, y_s.dtype)
        acc_s[...] = jnp.zeros(acc_s.shape, acc_s.dtype)

    xb = x_ref[0].astype(bf)
    xh = jnp.concatenate([xp_ref[0], xn_ref[0]], axis=0).astype(bf)
    kv = _bdot(xb, w_in_ref[:, K_OFF:GU_OFF])
    kvh = _bdot(xh, w_in_ref[:, K_OFF:GU_OFF])
    q_s[...] = (_bdot(xb, w_in_ref[:, Q_OFF:K_OFF]) * (LOG2E * HEAD_DIM ** -0.5)).astype(bf)
    gu_s[...] = _bdot(xb, w_in_ref[:, GU_OFF:GV_OFF])
    gv_s[...] = _bdot(xb, w_in_ref[:, GV_OFF:IN_WIDTH])
    _store_kv(k_s, v_s, 0, kvh[:BLOCK])
    _store_kv(k_s, v_s, BLOCK, kv)
    _store_kv(k_s, v_s, BLOCK + tile, kvh[BLOCK:])

    o_ref[0] = _layer_norm(acc_s[...], ln2g_ref[...], ln2b_ref[...])
    x1 = _layer_norm(y_s[...], ln1g_ref[...], ln1b_ref[...])
    x1_s[...] = x1
    x1b_s[...] = x1.astype(bf)

    hidden = {}

    def mlp_hidden(c):
        h = jnp.maximum(_bdot(x1b_s[...], w1_ref[:, c * FFN_CHUNK:(c + 1) * FFN_CHUNK]), 0.0)
        hidden[c] = (h * h).astype(bf)

    def mlp_output(c):
        part = _bdot(hidden.pop(c), w2_ref[c * FFN_CHUNK:(c + 1) * FFN_CHUNK, :])
        if c == 0:
            acc_s[...] = alpha * x1_s[...] + part
        else:
            acc_s[...] += part

    def out_projection(j):
        r0, r1 = j * BLOCK, (j + 1) * BLOCK
        y_s[r0:r1, :] = alpha * x_ref[0, r0:r1, :] + _bdot(mix_s[r0:r1, :], w_out_ref[...])

    plan = iter(FILL_PLAN)

    def fill():
        for item in (next(plan) or "").split():
            {"h": mlp_hidden, "o": mlp_output, "w": out_projection}[item[0]](int(item[1:]))

    seq_tile = jnp.clip(step, 0, last_tile) % n_tiles
    _mixer_outputs(seq_tile, n_tiles, sink_ref, lng_ref, lnb_ref, ws_ref, bs_ref, ng_ref,
                   q_s, k_s, v_s, gu_s, gv_s, bias_s, mix_s, fill)
    fill()


def _const_spec(shape):
    zeros = (0,) * len(shape)
    return pl.BlockSpec(shape, lambda *_: zeros, pipeline_mode=pl.Buffered(1))


def _block_call(x, sink, consts, alpha):
    batch, seq, d = x.shape
    n_tiles = seq // TILE
    n_steps = batch * n_tiles + PIPELINE_DEPTH - 1
    last_tile = batch * n_tiles - 1
    bpt = TILE // BLOCK
    n_seq_blocks = seq // BLOCK

    def tile_of(s, lag):
        t = jnp.clip(s - lag, 0, last_tile)
        return t // n_tiles, t % n_tiles

    def x_map(s):
        b, i = tile_of(s, 0)
        return b, i, 0

    def prev_map(s):
        b, i = tile_of(s, 0)
        return b, jnp.maximum(i * bpt - 1, 0), 0

    def next_map(s):
        b, i = tile_of(s, 0)
        return b, jnp.minimum((i + 1) * bpt, n_seq_blocks - 1), 0

    def out_map(s):
        b, i = tile_of(s, PIPELINE_DEPTH - 1)
        return b, i, 0

    kern = functools.partial(_block_kernel, alpha=alpha, n_tiles=n_tiles, n_steps=n_steps)
    in_specs = [
        pl.BlockSpec(memory_space=pltpu.SMEM),
        pl.BlockSpec((1, TILE, d), x_map),
        pl.BlockSpec((1, BLOCK, d), prev_map),
        pl.BlockSpec((1, BLOCK, d), next_map),
    ] + [_const_spec(c.shape) for c in consts]
    bf = jnp.bfloat16
    scratch = [
        pltpu.VMEM((TILE, ATTN_WIDTH), bf),
        pltpu.VMEM((4, TILE + 2 * BLOCK, LANES), bf),
        pltpu.VMEM((4, LANES, TILE + 2 * BLOCK), bf),
        pltpu.VMEM((TILE, GMLP_WIDTH), jnp.float32),
        pltpu.VMEM((TILE, GMLP_WIDTH), jnp.float32),
        pltpu.VMEM((N_Q_HEADS, 3 * BLOCK, BLOCK), jnp.float32),
        pltpu.VMEM((TILE, ATTN_WIDTH + GMLP_WIDTH), bf),
        pltpu.VMEM((TILE, d), jnp.float32),
        pltpu.VMEM((TILE, d), jnp.float32),
        pltpu.VMEM((TILE, d), bf),
        pltpu.VMEM((TILE, d), jnp.float32),
    ]
    return pl.pallas_call(
        kern,
        out_shape=jax.ShapeDtypeStruct(x.shape, x.dtype),
        grid=(n_steps,),
        in_specs=in_specs,
        out_specs=pl.BlockSpec((1, TILE, d), out_map),
        scratch_shapes=scratch,
        compiler_params=pltpu.CompilerParams(
            dimension_semantics=("arbitrary",),
            vmem_limit_bytes=VMEM_LIMIT_BYTES),
        name="block",
    )(sink, x, x, x, *consts)


def kernel(x, w_in, sink, gmlp_ln_g, gmlp_ln_b, w_spatial, b_spatial, attn_norm_g, gmlp_norm_g,
           w_out, ln1_g, ln1_b, w_ff1, w_ff2, ln2_g, ln2_b):
    batch, seq, d = x.shape
    depth = w_in.shape[0]
    assert seq % TILE == 0
    assert w_in.shape[2] == IN_WIDTH and w_spatial.shape[1:] == (GMLP_GROUPS, BLOCK, BLOCK)
    alpha = (2.0 * depth) ** 0.25
    bf = jnp.bfloat16
    row = lambda a: a.reshape(1, -1)
    for l in range(depth):
        ws_cat = w_spatial[l].reshape(GMLP_GROUPS // 2, 2, BLOCK, BLOCK)
        ws_cat = ws_cat.transpose(0, 2, 1, 3).reshape(GMLP_GROUPS // 2, BLOCK, 2 * BLOCK).astype(bf)
        bs_full = jnp.repeat(b_spatial[l].T, GMLP_WIDTH // GMLP_GROUPS, axis=1)
        norm_g = jnp.concatenate([attn_norm_g[l], gmlp_norm_g[l]]).reshape(1, -1)
        consts = (w_in[l].astype(bf), row(gmlp_ln_g[l]), row(gmlp_ln_b[l]), ws_cat, bs_full,
                  norm_g, w_out[l].astype(bf), row(ln1_g[l]), row(ln1_b[l]),
                  w_ff1[l].astype(bf), w_ff2[l].astype(bf), row(ln2_g[l]), row(ln2_b[l]))
        x = _block_call(x, sink[l], consts, alpha)
    return x
```

```python
import functools
import math

import jax
import jax.numpy as jnp
from jax import lax
from jax.experimental import pallas as pl
from jax.experimental.pallas import tpu as pltpu

HEAD_DIM = 64
N_Q_HEADS = 8
N_KV_HEADS = 2
Q_PER_KV = N_Q_HEADS // N_KV_HEADS
ATTN_WIDTH = N_Q_HEADS * HEAD_DIM
KV_WIDTH = N_KV_HEADS * HEAD_DIM
GMLP_WIDTH = 512
GMLP_GROUPS = 8
WINDOW = 128
BLOCK = 128
LN_EPS = 1e-5
NEG_INF = -1e30
LOG2E = math.log2(math.e)
LANES = 128

Q_OFF = 0
K_OFF = ATTN_WIDTH
V_OFF = ATTN_WIDTH + KV_WIDTH
GU_OFF = ATTN_WIDTH + 2 * KV_WIDTH
GV_OFF = GU_OFF + GMLP_WIDTH
IN_WIDTH = GV_OFF + GMLP_WIDTH

TILE = 512
FFN_CHUNK = 512
PIPELINE_DEPTH = 3
VMEM_LIMIT_BYTES = 60 * 1024 * 1024


def _bdot(a, b):
    return jnp.dot(a, b, preferred_element_type=jnp.float32)


def _gelu(x):
    return 0.5 * x * (1.0 + lax.erf(x * (0.5 ** 0.5)))


def _layer_norm(y, g, b):
    mu = jnp.mean(y, axis=-1, keepdims=True)
    yc = y - mu
    var = jnp.mean(yc * yc, axis=-1, keepdims=True)
    return yc * lax.rsqrt(var + LN_EPS) * g + b


def _rms_norm(y, g):
    ms = jnp.mean(y * y, axis=-1, keepdims=True)
    return y * lax.rsqrt(ms + LN_EPS) * g


def _half_masked(a, left):
    sw = pltpu.roll(a, HEAD_DIM, axis=1)
    zero = jnp.zeros_like(a)
    bf = jnp.bfloat16
    return (jnp.where(left, a, zero).astype(bf), jnp.where(left, zero, sw).astype(bf),
            jnp.where(left, sw, zero).astype(bf), jnp.where(left, zero, a).astype(bf))


def _fill_bias_table(bias_s):
    key = lax.broadcasted_iota(jnp.int32, (3 * BLOCK, BLOCK), 0)
    query = lax.broadcasted_iota(jnp.int32, (3 * BLOCK, BLOCK), 1)
    dist = jnp.abs(key - BLOCK - query)
    distf = dist.astype(jnp.float32)
    for h in range(N_Q_HEADS):
        slope = 2.0 ** (-8.0 * (h + 1) / N_Q_HEADS)
        bias_s[h] = jnp.where(dist <= WINDOW, (-slope * LOG2E) * distf, NEG_INF)


def _store_kv(k_s, v_s, start, kv_rows):
    n = kv_rows.shape[0]
    bf = jnp.bfloat16
    left = lax.broadcasted_iota(jnp.int32, (n, LANES), 1) < HEAD_DIM
    for i, a in enumerate(_half_masked(kv_rows[:, :KV_WIDTH], left)):
        k_s[i, start:start + n, :] = a
    vt = kv_rows[:, KV_WIDTH:].T
    vt_sw = pltpu.roll(vt, HEAD_DIM, axis=0)
    top = lax.broadcasted_iota(jnp.int32, (KV_WIDTH, n), 0) < HEAD_DIM
    zero = jnp.zeros_like(vt)
    for i, a in enumerate((jnp.where(top, vt, zero), jnp.where(top, zero, vt_sw),
                           jnp.where(top, vt_sw, zero), jnp.where(top, zero, vt))):
        v_s[i, :, start:start + n] = a.astype(bf)


def _mixer_outputs(seq_tile, n_tiles, sink_ref, lng_ref, lnb_ref, ws_ref, bs_ref, ng_ref,
                   q_s, k_s, v_s, gu_s, gv_s, bias_s, mix_s, fill):
    n_blocks = q_s.shape[0] // BLOCK
    bf = jnp.bfloat16
    lane_left = lax.broadcasted_iota(jnp.int32, (BLOCK, BLOCK), 1) < HEAD_DIM
    has_prev = seq_tile != 0
    has_next = seq_tile != n_tiles - 1

    for j in range(n_blocks):
        r0, r1 = j * BLOCK, (j + 1) * BLOCK
        k0, k1 = j * BLOCK, (j + 3) * BLOCK
        attn_parts = []
        for g in range(N_KV_HEADS):
            lhs = jnp.concatenate([q_s[r0:r1, g * 256:g * 256 + LANES],
                                   q_s[r0:r1, g * 256 + LANES:(g + 1) * 256]], axis=0)
            kk = jnp.concatenate([k_s[2 * g, k0:k1, :], k_s[2 * g + 1, k0:k1, :]], axis=0)
            st = lax.dot_general(kk, lhs, (((1,), (1,)), ((), ())),
                                 preferred_element_type=jnp.float32)
            fill()
            p_par = []
            for par in range(2):
                p_half = []
                for half in range(2):
                    h = Q_PER_KV * g + 2 * half + par
                    sink = sink_ref[h] * LOG2E
                    sb = st[par * 3 * BLOCK:(par + 1) * 3 * BLOCK,
                            half * BLOCK:(half + 1) * BLOCK] + bias_s[h]
                    if j == 0:
                        sb = jnp.concatenate(
                            [jnp.where(has_prev, sb[:BLOCK], NEG_INF), sb[BLOCK:]], axis=0)
                    if j == n_blocks - 1:
                        sb = jnp.concatenate(
                            [sb[:2 * BLOCK], jnp.where(has_next, sb[2 * BLOCK:], NEG_INF)], axis=0)
                    m = jnp.maximum(jnp.max(sb, axis=0, keepdims=True), sink)
                    e = jnp.exp2(sb - m)
                    den = jnp.sum(e, axis=0, keepdims=True) + jnp.exp2(sink - m)
                    p_half.append((e * (1.0 / den)).astype(bf))
                p_par.append(jnp.concatenate(p_half, axis=1))
            pt = jnp.concatenate(p_par, axis=0)
            vvt = jnp.concatenate([v_s[2 * g, :, k0:k1], v_s[2 * g + 1, :, k0:k1]], axis=1)
            ot = _bdot(vvt, pt)
            fill()
            attn_parts += [ot[:, :BLOCK].T, ot[:, BLOCK:].T]
        attn = jnp.concatenate(attn_parts, axis=1)
        mix_s[r0:r1, 0:ATTN_WIDTH] = _rms_norm(attn, ng_ref[:, 0:ATTN_WIDTH]).astype(bf)

        vn = _layer_norm(_gelu(gv_s[r0:r1, :]), lng_ref[...], lnb_ref[...])
        mixed_parts = []
        for pair in range(GMLP_GROUPS // 2):
            vp = vn[:, pair * LANES:(pair + 1) * LANES]
            zero = jnp.zeros_like(vp)
            rhs = jnp.concatenate([jnp.where(lane_left, vp, zero),
                                   jnp.where(lane_left, zero, vp)], axis=0).astype(bf)
            mixed_parts.append(_bdot(ws_ref[pair], rhs))
        fill()
        mixed = jnp.concatenate(mixed_parts, axis=1) + bs_ref[...]
        sgu = _gelu(gu_s[r0:r1, :]) * mixed
        mix_s[r0:r1, ATTN_WIDTH:] = _rms_norm(sgu, ng_ref[:, ATTN_WIDTH:]).astype(bf)


HOOKS_PER_BLOCK = 5

FILL_PLAN = (
    "h0", None, "h1", None, "h2",
    "w0 o0", None, "h3", None, "h4",
    "w1 o1", None, "h5", None, "h6",
    "w2 h7", None, "o2", None, "o3",
    "w3",
)


def _block_kernel(sink_ref, x_ref, xp_ref, xn_ref, w_in_ref, lng_ref, lnb_ref, ws_ref, bs_ref,
                  ng_ref, w_out_ref, ln1g_ref, ln1b_ref, w1_ref, w2_ref, ln2g_ref, ln2b_ref,
                  o_ref, q_s, k_s, v_s, gu_s, gv_s, bias_s, mix_s, y_s, x1_s, x1b_s, acc_s,
                  *, alpha, n_tiles, n_steps):
    step = pl.program_id(0)
    last_tile = n_steps - PIPELINE_DEPTH
    tile = x_ref.shape[1]
    bf = jnp.bfloat16
    assert len(FILL_PLAN) == HOOKS_PER_BLOCK * (tile // BLOCK) + 1
    assert w1_ref.shape[1] == 8 * FFN_CHUNK and tile == 4 * BLOCK

    @pl.when(step == 0)
    def _():
        _fill_bias_table(bias_s)
        y_s[...] = jnp.zeros(y_s.shape, y_s.dtype)
        acc_s[...] = jnp.zeros(acc_s.shape, acc_s.dtype)

    xb = x_ref[0].astype(bf)
    xh = jnp.concatenate([xp_ref[0], xn_ref[0]], axis=0).astype(bf)
    kv = _bdot(xb, w_in_ref[:, K_OFF:GU_OFF])
    kvh = _bdot(xh, w_in_ref[:, K_OFF:GU_OFF])
    q_s[...] = (_bdot(xb, w_in_ref[:, Q_OFF:K_OFF]) * (LOG2E * HEAD_DIM ** -0.5)).astype(bf)
    gu_s[...] = _bdot(xb, w_in_ref[:, GU_OFF:GV_OFF])
    gv_s[...] = _bdot(xb, w_in_ref[:, GV_OFF:IN_WIDTH])
    _store_kv(k_s, v_s, 0, kvh[:BLOCK])
    _store_kv(k_s, v_s, BLOCK, kv)
    _store_kv(k_s, v_s, BLOCK + tile, kvh[BLOCK:])

    o_ref[0] = _layer_norm(acc_s[...], ln2g_ref[...], ln2b_ref[...])
    x1 = _layer_norm(y_s[...], ln1g_ref[...], ln1b_ref[...])
    x1_s[...] = x1
    x1b_s[...] = x1.astype(bf)

    hidden = {}

    def mlp_hidden(c):
        h = jnp.maximum(_bdot(x1b_s[...], w1_ref[:, c * FFN_CHUNK:(c + 1) * FFN_CHUNK]), 0.0)
        hidden[c] = (h * h).astype(bf)

    def mlp_output(c):
        h2 = jnp.concatenate([hidden.pop(2 * c), hidden.pop(2 * c + 1)], axis=1)
        part = _bdot(h2, w2_ref[2 * c * FFN_CHUNK:(2 * c + 2) * FFN_CHUNK, :])
        if c == 0:
            acc_s[...] = alpha * x1_s[...] + part
        else:
            acc_s[...] += part

    def out_projection(j):
        r0, r1 = j * BLOCK, (j + 1) * BLOCK
        y_s[r0:r1, :] = alpha * x_ref[0, r0:r1, :] + _bdot(mix_s[r0:r1, :], w_out_ref[...])

    plan = iter(FILL_PLAN)

    def fill():
        for item in (next(plan) or "").split():
            {"h": mlp_hidden, "o": mlp_output, "w": out_projection}[item[0]](int(item[1:]))

    seq_tile = jnp.clip(step, 0, last_tile) % n_tiles
    _mixer_outputs(seq_tile, n_tiles, sink_ref, lng_ref, lnb_ref, ws_ref, bs_ref, ng_ref,
                   q_s, k_s, v_s, gu_s, gv_s, bias_s, mix_s, fill)
    fill()


def _const_spec(shape):
    zeros = (0,) * len(shape)
    return pl.BlockSpec(shape, lambda *_: zeros, pipeline_mode=pl.Buffered(1))


def _block_call(x, sink, consts, alpha):
    batch, seq, d = x.shape
    n_tiles = seq // TILE
    n_steps = batch * n_tiles + PIPELINE_DEPTH - 1
    last_tile = batch * n_tiles - 1
    bpt = TILE // BLOCK
    n_seq_blocks = seq // BLOCK

    def tile_of(s, lag):
        t = jnp.clip(s - lag, 0, last_tile)
        return t // n_tiles, t % n_tiles

    def x_map(s):
        b, i = tile_of(s, 0)
        return b, i, 0

    def prev_map(s):
        b, i = tile_of(s, 0)
        return b, jnp.maximum(i * bpt - 1, 0), 0

    def next_map(s):
        b, i = tile_of(s, 0)
        return b, jnp.minimum((i + 1) * bpt, n_seq_blocks - 1), 0

    def out_map(s):
        b, i = tile_of(s, PIPELINE_DEPTH - 1)
        return b, i, 0

    kern = functools.partial(_block_kernel, alpha=alpha, n_tiles=n_tiles, n_steps=n_steps)
    in_specs = [
        pl.BlockSpec(memory_space=pltpu.SMEM),
        pl.BlockSpec((1, TILE, d), x_map),
        pl.BlockSpec((1, BLOCK, d), prev_map),
        pl.BlockSpec((1, BLOCK, d), next_map),
    ] + [_const_spec(c.shape) for c in consts]
    bf = jnp.bfloat16
    scratch = [
        pltpu.VMEM((TILE, ATTN_WIDTH), bf),
        pltpu.VMEM((4, TILE + 2 * BLOCK, LANES), bf),
        pltpu.VMEM((4, LANES, TILE + 2 * BLOCK), bf),
        pltpu.VMEM((TILE, GMLP_WIDTH), jnp.float32),
        pltpu.VMEM((TILE, GMLP_WIDTH), jnp.float32),
        pltpu.VMEM((N_Q_HEADS, 3 * BLOCK, BLOCK), jnp.float32),
        pltpu.VMEM((TILE, ATTN_WIDTH + GMLP_WIDTH), bf),
        pltpu.VMEM((TILE, d), jnp.float32),
        pltpu.VMEM((TILE, d), jnp.float32),
        pltpu.VMEM((TILE, d), bf),
        pltpu.VMEM((TILE, d), jnp.float32),
    ]
    return pl.pallas_call(
        kern,
        out_shape=jax.ShapeDtypeStruct(x.shape, x.dtype),
        grid=(n_steps,),
        in_specs=in_specs,
        out_specs=pl.BlockSpec((1, TILE, d), out_map),
        scratch_shapes=scratch,
        compiler_params=pltpu.CompilerParams(
            dimension_semantics=("arbitrary",),
            vmem_limit_bytes=VMEM_LIMIT_BYTES),
        name="block",
    )(sink, x, x, x, *consts)


def kernel(x, w_in, sink, gmlp_ln_g, gmlp_ln_b, w_spatial, b_spatial, attn_norm_g, gmlp_norm_g,
           w_out, ln1_g, ln1_b, w_ff1, w_ff2, ln2_g, ln2_b):
    batch, seq, d = x.shape
    depth = w_in.shape[0]
    assert seq % TILE == 0
    assert w_in.shape[2] == IN_WIDTH and w_spatial.shape[1:] == (GMLP_GROUPS, BLOCK, BLOCK)
    alpha = (2.0 * depth) ** 0.25
    bf = jnp.bfloat16
    row = lambda a: a.reshape(1, -1)
    for l in range(depth):
        ws_cat = w_spatial[l].reshape(GMLP_GROUPS // 2, 2, BLOCK, BLOCK)
        ws_cat = ws_cat.transpose(0, 2, 1, 3).reshape(GMLP_GROUPS // 2, BLOCK, 2 * BLOCK).astype(bf)
        bs_full = jnp.repeat(b_spatial[l].T, GMLP_WIDTH // GMLP_GROUPS, axis=1)
        norm_g = jnp.concatenate([attn_norm_g[l], gmlp_norm_g[l]]).reshape(1, -1)
        consts = (w_in[l].astype(bf), row(gmlp_ln_g[l]), row(gmlp_ln_b[l]), ws_cat, bs_full,
                  norm_g, w_out[l].astype(bf), row(ln1_g[l]), row(ln1_b[l]),
                  w_ff1[l].astype(bf), w_ff2[l].astype(bf), row(ln2_g[l]), row(ln2_b[l]))
        x = _block_call(x, sink[l], consts, alpha)
    return x
```

```python
import functools
import math

import jax
import jax.numpy as jnp
from jax import lax
from jax.experimental import pallas as pl
from jax.experimental.pallas import tpu as pltpu

HEAD_DIM = 64
N_Q_HEADS = 8
N_KV_HEADS = 2
Q_PER_KV = N_Q_HEADS // N_KV_HEADS
ATTN_WIDTH = N_Q_HEADS * HEAD_DIM
KV_WIDTH = N_KV_HEADS * HEAD_DIM
GROUP_WIDTH = Q_PER_KV * HEAD_DIM
GMLP_WIDTH = 512
GMLP_GROUPS = 8
WINDOW = 128
BLOCK = 128
LN_EPS = 1e-5
NEG_INF = -1e30
LOG2E = math.log2(math.e)
LANES = 128

Q_OFF = 0
K_OFF = ATTN_WIDTH
GU_OFF = ATTN_WIDTH + 2 * KV_WIDTH
GV_OFF = GU_OFF + GMLP_WIDTH
IN_WIDTH = GV_OFF + GMLP_WIDTH

TILE = 512
FFN_CHUNK = 512
PIPELINE_DEPTH = 3
V7X_VMEM_BYTES = 64 * 1024 * 1024
VMEM_LIMIT_BYTES = V7X_VMEM_BYTES - 8 * 1024 * 1024


def _bdot(a, b):
    return jnp.dot(a, b, preferred_element_type=jnp.float32)


def _gelu(x):
    return 0.5 * x * (1.0 + lax.erf(x * (0.5 ** 0.5)))


def _layer_norm(y, g, b):
    mu = jnp.mean(y, axis=-1, keepdims=True)
    yc = y - mu
    var = jnp.mean(yc * yc, axis=-1, keepdims=True)
    return yc * lax.rsqrt(var + LN_EPS) * g + b


def _rms_norm(y, g):
    ms = jnp.mean(y * y, axis=-1, keepdims=True)
    return y * lax.rsqrt(ms + LN_EPS) * g


def _half_masked(a, left):
    sw = pltpu.roll(a, HEAD_DIM, axis=1)
    zero = jnp.zeros_like(a)
    bf = jnp.bfloat16
    return (jnp.where(left, a, zero).astype(bf), jnp.where(left, zero, sw).astype(bf),
            jnp.where(left, sw, zero).astype(bf), jnp.where(left, zero, a).astype(bf))


def _fill_bias_table(bias_s):
    key = lax.broadcasted_iota(jnp.int32, (3 * BLOCK, BLOCK), 0)
    query = lax.broadcasted_iota(jnp.int32, (3 * BLOCK, BLOCK), 1)
    dist = jnp.abs(key - BLOCK - query)
    distf = dist.astype(jnp.float32)
    for h in range(N_Q_HEADS):
        slope = 2.0 ** (-8.0 * (h + 1) / N_Q_HEADS)
        bias_s[h] = jnp.where(dist <= WINDOW, (-slope * LOG2E) * distf, NEG_INF)


def _store_kv(k_s, v_s, start, kv_rows):
    n = kv_rows.shape[0]
    bf = jnp.bfloat16
    left = lax.broadcasted_iota(jnp.int32, (n, LANES), 1) < HEAD_DIM
    for i, a in enumerate(_half_masked(kv_rows[:, :KV_WIDTH], left)):
        k_s[i, start:start + n, :] = a
    vt = kv_rows[:, KV_WIDTH:].T
    vt_sw = pltpu.roll(vt, HEAD_DIM, axis=0)
    top = lax.broadcasted_iota(jnp.int32, (KV_WIDTH, n), 0) < HEAD_DIM
    zero = jnp.zeros_like(vt)
    for i, a in enumerate((jnp.where(top, vt, zero), jnp.where(top, zero, vt_sw),
                           jnp.where(top, vt_sw, zero), jnp.where(top, zero, vt))):
        v_s[i, :, start:start + n] = a.astype(bf)


def _mixer_outputs(seq_tile, n_tiles, sink_ref, lng_ref, lnb_ref, ws_ref, bs_ref, ng_ref,
                   q_s, k_s, v_s, gu_s, gv_s, bias_s, mix_s, fill):
    n_blocks = q_s.shape[0] // BLOCK
    bf = jnp.bfloat16
    lane_left = lax.broadcasted_iota(jnp.int32, (BLOCK, BLOCK), 1) < HEAD_DIM
    has_prev = seq_tile != 0
    has_next = seq_tile != n_tiles - 1

    for j in range(n_blocks):
        r0, r1 = j * BLOCK, (j + 1) * BLOCK
        k0, k1 = j * BLOCK, (j + 3) * BLOCK
        attn_parts = []
        for g in range(N_KV_HEADS):
            c0 = g * GROUP_WIDTH
            lhs = jnp.concatenate([q_s[r0:r1, c0:c0 + LANES],
                                   q_s[r0:r1, c0 + LANES:c0 + GROUP_WIDTH]], axis=0)
            kk = jnp.concatenate([k_s[2 * g, k0:k1, :], k_s[2 * g + 1, k0:k1, :]], axis=0)
            st = lax.dot_general(kk, lhs, (((1,), (1,)), ((), ())),
                                 preferred_element_type=jnp.float32)
            fill()
            p_par = []
            for par in range(2):
                p_half = []
                for half in range(2):
                    h = Q_PER_KV * g + 2 * half + par
                    sink = sink_ref[h] * LOG2E
                    sb = st[par * 3 * BLOCK:(par + 1) * 3 * BLOCK,
                            half * BLOCK:(half + 1) * BLOCK] + bias_s[h]
                    if j == 0:
                        sb = jnp.concatenate(
                            [jnp.where(has_prev, sb[:BLOCK], NEG_INF), sb[BLOCK:]], axis=0)
                    if j == n_blocks - 1:
                        sb = jnp.concatenate(
                            [sb[:2 * BLOCK], jnp.where(has_next, sb[2 * BLOCK:], NEG_INF)], axis=0)
                    m = jnp.maximum(jnp.max(sb, axis=0, keepdims=True), sink)
                    e = jnp.exp2(sb - m)
                    den = jnp.sum(e, axis=0, keepdims=True) + jnp.exp2(sink - m)
                    p_half.append((e * (1.0 / den)).astype(bf))
                p_par.append(jnp.concatenate(p_half, axis=1))
            pt = jnp.concatenate(p_par, axis=0)
            vvt = jnp.concatenate([v_s[2 * g, :, k0:k1], v_s[2 * g + 1, :, k0:k1]], axis=1)
            ot = _bdot(vvt, pt)
            fill()
            attn_parts += [ot[:, :BLOCK].T, ot[:, BLOCK:].T]
        attn = jnp.concatenate(attn_parts, axis=1)
        mix_s[r0:r1, 0:ATTN_WIDTH] = _rms_norm(attn, ng_ref[:, 0:ATTN_WIDTH]).astype(bf)

        vn = _layer_norm(_gelu(gv_s[r0:r1, :]), lng_ref[...], lnb_ref[...])
        mixed_parts = []
        for pair in range(GMLP_GROUPS // 2):
            vp = vn[:, pair * LANES:(pair + 1) * LANES]
            zero = jnp.zeros_like(vp)
            rhs = jnp.concatenate([jnp.where(lane_left, vp, zero),
                                   jnp.where(lane_left, zero, vp)], axis=0).astype(bf)
            mixed_parts.append(_bdot(ws_ref[pair], rhs))
        fill()
        mixed = jnp.concatenate(mixed_parts, axis=1) + bs_ref[...]
        sgu = _gelu(gu_s[r0:r1, :]) * mixed
        mix_s[r0:r1, ATTN_WIDTH:] = _rms_norm(sgu, ng_ref[:, ATTN_WIDTH:]).astype(bf)


HOOKS_PER_BLOCK = 5
PLAN_BLOCKS = 4
PLAN_MLP_CHUNKS = 8

FILL_PLAN = (
    "h0", None, "h1", None, "h2",
    "w0 o0", None, "h3", None, "h4",
    "w1 o1", None, "h5", None, "h6",
    "w2 h7", None, "o2", None, "o3",
    "w3",
)


def _block_kernel(sink_ref, x_ref, xp_ref, xn_ref, w_in_ref, lng_ref, lnb_ref, ws_ref, bs_ref,
                  ng_ref, w_out_ref, ln1g_ref, ln1b_ref, w1_ref, w2_ref, ln2g_ref, ln2b_ref,
                  o_ref, q_s, k_s, v_s, gu_s, gv_s, bias_s, mix_s, y_s, x1_s, x1b_s, acc_s,
                  *, alpha, n_tiles, n_steps):
    step = pl.program_id(0)
    last_tile = n_steps - PIPELINE_DEPTH
    tile = x_ref.shape[1]
    bf = jnp.bfloat16
    assert len(FILL_PLAN) == HOOKS_PER_BLOCK * PLAN_BLOCKS + 1
    assert w1_ref.shape[1] == PLAN_MLP_CHUNKS * FFN_CHUNK and tile == PLAN_BLOCKS * BLOCK

    @pl.when(step == 0)
    def _():
        _fill_bias_table(bias_s)
        y_s[...] = jnp.zeros(y_s.shape, y_s.dtype)
        acc_s[...] = jnp.zeros(acc_s.shape, acc_s.dtype)

    xb = x_ref[0].astype(bf)
    xh = jnp.concatenate([xp_ref[0], xn_ref[0]], axis=0).astype(bf)
    kv = _bdot(xb, w_in_ref[:, K_OFF:GU_OFF])
    kvh = _bdot(xh, w_in_ref[:, K_OFF:GU_OFF])
    q_s[...] = (_bdot(xb, w_in_ref[:, Q_OFF:K_OFF]) * (LOG2E * HEAD_DIM ** -0.5)).astype(bf)
    gu_s[...] = _bdot(xb, w_in_ref[:, GU_OFF:GV_OFF])
    gv_s[...] = _bdot(xb, w_in_ref[:, GV_OFF:IN_WIDTH])
    _store_kv(k_s, v_s, 0, kvh[:BLOCK])
    _store_kv(k_s, v_s, BLOCK, kv)
    _store_kv(k_s, v_s, BLOCK + tile, kvh[BLOCK:])

    o_ref[0] = _layer_norm(acc_s[...], ln2g_ref[...], ln2b_ref[...])
    x1 = _layer_norm(y_s[...], ln1g_ref[...], ln1b_ref[...])
    x1_s[...] = x1
    x1b_s[...] = x1.astype(bf)

    hidden = {}

    def mlp_hidden(c):
        h = jnp.maximum(_bdot(x1b_s[...], w1_ref[:, c * FFN_CHUNK:(c + 1) * FFN_CHUNK]), 0.0)
        hidden[c] = (h * h).astype(bf)

    def mlp_output(c):
        h2 = jnp.concatenate([hidden.pop(2 * c), hidden.pop(2 * c + 1)], axis=1)
        part = _bdot(h2, w2_ref[2 * c * FFN_CHUNK:(2 * c + 2) * FFN_CHUNK, :])
        if c == 0:
            acc_s[...] = alpha * x1_s[...] + part
        else:
            acc_s[...] += part

    def out_projection(j):
        r0, r1 = j * BLOCK, (j + 1) * BLOCK
        y_s[r0:r1, :] = alpha * x_ref[0, r0:r1, :] + _bdot(mix_s[r0:r1, :], w_out_ref[...])

    plan = iter(FILL_PLAN)

    def fill():
        for item in (next(plan) or "").split():
            {"h": mlp_hidden, "o": mlp_output, "w": out_projection}[item[0]](int(item[1:]))

    seq_tile = jnp.clip(step, 0, last_tile) % n_tiles
    _mixer_outputs(seq_tile, n_tiles, sink_ref, lng_ref, lnb_ref, ws_ref, bs_ref, ng_ref,
                   q_s, k_s, v_s, gu_s, gv_s, bias_s, mix_s, fill)
    fill()


def _const_spec(shape):
    zeros = (0,) * len(shape)
    return pl.BlockSpec(shape, lambda *_: zeros, pipeline_mode=pl.Buffered(1))


def _block_call(x, sink, consts, alpha):
    batch, seq, d = x.shape
    n_tiles = seq // TILE
    n_steps = batch * n_tiles + PIPELINE_DEPTH - 1
    last_tile = batch * n_tiles - 1
    bpt = TILE // BLOCK
    n_seq_blocks = seq // BLOCK

    def tile_of(s, lag):
        t = jnp.clip(s - lag, 0, last_tile)
        return t // n_tiles, t % n_tiles

    def x_map(s):
        b, i = tile_of(s, 0)
        return b, i, 0

    def prev_map(s):
        b, i = tile_of(s, 0)
        return b, jnp.maximum(i * bpt - 1, 0), 0

    def next_map(s):
        b, i = tile_of(s, 0)
        return b, jnp.minimum((i + 1) * bpt, n_seq_blocks - 1), 0

    def out_map(s):
        b, i = tile_of(s, PIPELINE_DEPTH - 1)
        return b, i, 0

    kern = functools.partial(_block_kernel, alpha=alpha, n_tiles=n_tiles, n_steps=n_steps)
    in_specs = [
        pl.BlockSpec(memory_space=pltpu.SMEM),
        pl.BlockSpec((1, TILE, d), x_map),
        pl.BlockSpec((1, BLOCK, d), prev_map),
        pl.BlockSpec((1, BLOCK, d), next_map),
    ] + [_const_spec(c.shape) for c in consts]
    bf = jnp.bfloat16
    scratch = [
        pltpu.VMEM((TILE, ATTN_WIDTH), bf),
        pltpu.VMEM((4, TILE + 2 * BLOCK, LANES), bf),
        pltpu.VMEM((4, LANES, TILE + 2 * BLOCK), bf),
        pltpu.VMEM((TILE, GMLP_WIDTH), jnp.float32),
        pltpu.VMEM((TILE, GMLP_WIDTH), jnp.float32),
        pltpu.VMEM((N_Q_HEADS, 3 * BLOCK, BLOCK), jnp.float32),
        pltpu.VMEM((TILE, ATTN_WIDTH + GMLP_WIDTH), bf),
        pltpu.VMEM((TILE, d), jnp.float32),
        pltpu.VMEM((TILE, d), jnp.float32),
        pltpu.VMEM((TILE, d), bf),
        pltpu.VMEM((TILE, d), jnp.float32),
    ]
    return pl.pallas_call(
        kern,
        out_shape=jax.ShapeDtypeStruct(x.shape, x.dtype),
        grid=(n_steps,),
        in_specs=in_specs,
        out_specs=pl.BlockSpec((1, TILE, d), out_map),
        scratch_shapes=scratch,
        compiler_params=pltpu.CompilerParams(
            dimension_semantics=("arbitrary",),
            vmem_limit_bytes=VMEM_LIMIT_BYTES),
        name="block",
    )(sink, x, x, x, *consts)


def kernel(x, w_in, sink, gmlp_ln_g, gmlp_ln_b, w_spatial, b_spatial, attn_norm_g, gmlp_norm_g,
           w_out, ln1_g, ln1_b, w_ff1, w_ff2, ln2_g, ln2_b):
    batch, seq, d = x.shape
    depth = w_in.shape[0]
    assert seq % TILE == 0
    assert w_in.shape[2] == IN_WIDTH and w_spatial.shape[1:] == (GMLP_GROUPS, BLOCK, BLOCK)
    alpha = (2.0 * depth) ** 0.25
    bf = jnp.bfloat16
    row = lambda a: a.reshape(1, -1)
    for l in range(depth):
        ws_cat = w_spatial[l].reshape(GMLP_GROUPS // 2, 2, BLOCK, BLOCK)
        ws_cat = ws_cat.transpose(0, 2, 1, 3).reshape(GMLP_GROUPS // 2, BLOCK, 2 * BLOCK).astype(bf)
        bs_full = jnp.repeat(b_spatial[l].T, GMLP_WIDTH // GMLP_GROUPS, axis=1)
        norm_g = jnp.concatenate([attn_norm_g[l], gmlp_norm_g[l]]).reshape(1, -1)
        consts = (w_in[l].astype(bf), row(gmlp_ln_g[l]), row(gmlp_ln_b[l]), ws_cat, bs_full,
                  norm_g, w_out[l].astype(bf), row(ln1_g[l]), row(ln1_b[l]),
                  w_ff1[l].astype(bf), w_ff2[l].astype(bf), row(ln2_g[l]), row(ln2_b[l]))
        x = _block_call(x, sink[l], consts, alpha)
    return x
```
